```python
import math
import jax, jax.numpy as jnp
from jax import lax
import numpy as np

D_MODEL = 1024
BATCH = 8
SEQ = 2048
DEPTH = 2

HEAD_DIM = 64
ROPE_THETA = 500000.0
Q_BLOCK = 128
NORM_EPS = 1e-6

SSD_D_INNER = D_MODEL
SSD_HEAD_DIM = 64
SSD_HEADS = SSD_D_INNER // SSD_HEAD_DIM
SSD_GROUPS = 2
SSD_STATE = 128
SSD_CONV = 4
SSD_CHUNK = 128
SSD_CONV_CH = SSD_D_INNER + 2 * SSD_GROUPS * SSD_STATE

SB_HEADS = 8
SB_WIDTH = SB_HEADS * HEAD_DIM

DSA_HEADS = 8
DSA_KV_HEADS = 2
DSA_WIDTH = DSA_HEADS * HEAD_DIM
DSA_KV_WIDTH = DSA_KV_HEADS * HEAD_DIM
IDX_HEADS = 8
IDX_DIM = 64
DSA_MAX_TOPK = 256

N_BRANCH = 3
MLP_HIDDEN = 4 * D_MODEL

IN_SPLITS = (
    SSD_D_INNER,
    SSD_CONV_CH,
    SSD_HEADS,
    SB_WIDTH, SB_WIDTH, SB_WIDTH,
    DSA_WIDTH, DSA_KV_WIDTH, DSA_KV_WIDTH,
    IDX_HEADS * IDX_DIM, IDX_DIM, IDX_HEADS,
    N_BRANCH * D_MODEL,
)
IN_WIDTH = sum(IN_SPLITS)

kernel_name = "hybrid_ssd_stickbreak_dsa_block"


def rms_norm(x, w):
    xf = x.astype(jnp.float32)
    y = xf * lax.rsqrt(jnp.mean(xf * xf, axis=-1, keepdims=True) + NORM_EPS)
    return (y * w.astype(jnp.float32)).astype(x.dtype)


def partial_rope(x, pos):
    d = x.shape[-1]
    rot = d // 4
    half = rot // 2
    inv_freq = jnp.exp(jnp.arange(half, dtype=jnp.float32) * (-2.0 * math.log(ROPE_THETA) / rot))
    ang = pos.astype(jnp.float32)[:, None] * inv_freq[None, :]
    cos = jnp.cos(ang)[None, :, None, :]
    sin = jnp.sin(ang)[None, :, None, :]
    xf = x.astype(jnp.float32)
    x1, x2, rest = xf[..., :half], xf[..., half:rot], xf[..., rot:]
    out = jnp.concatenate([x1 * cos - x2 * sin, x2 * cos + x1 * sin, rest], axis=-1)
    return out.astype(x.dtype)


def causal_depthwise_conv(x, w, b):
    k = w.shape[0]
    y = lax.conv_general_dilated(
        x, w[:, None, :].astype(x.dtype), window_strides=(1,), padding=[(k - 1, 0)],
        dimension_numbers=("NWC", "WIO", "NWC"), feature_group_count=x.shape[-1])
    return y + b.astype(x.dtype)


def ssd_chunked_scan(x, dt, a, bm, cm):
    bsz, L, g, r, p = x.shape
    n = bm.shape[-1]
    nc = L // SSD_CHUNK
    x = x.reshape(bsz, nc, SSD_CHUNK, g, r, p)
    dt = dt.reshape(bsz, nc, SSD_CHUNK, g, r)
    bm = bm.reshape(bsz, nc, SSD_CHUNK, g, n)
    cm = cm.reshape(bsz, nc, SSD_CHUNK, g, n)
    a_cum = jnp.cumsum(dt * a, axis=2)
    xdt = x * dt[..., None]
    ac = jnp.moveaxis(a_cum, 2, -1)
    causal = jnp.tril(jnp.ones((SSD_CHUNK, SSD_CHUNK), dtype=bool))
    decay_ij = jnp.exp(jnp.where(causal, ac[..., :, None] - ac[..., None, :], -jnp.inf))
    cb = jnp.einsum("bcign,bcjgn->bcgij", cm, bm)
    y_diag = jnp.einsum("bcgrij,bcjgrp->bcigrp", cb[:, :, :, None] * decay_ij, xdt)
    decay_to_end = jnp.exp(a_cum[:, :, -1:] - a_cum)
    states = jnp.einsum("bcjgn,bcjgrp->bcgrpn", bm, xdt * decay_to_end[..., None])
    chunk_decay = jnp.exp(a_cum[:, :, -1])

    def step(h, inp):
        s_c, d_c = inp
        return d_c[..., None, None] * h + s_c, h

    h0 = jnp.zeros((bsz, g, r, p, n), states.dtype)
    _, prev = lax.scan(step, h0, (jnp.moveaxis(states, 1, 0), jnp.moveaxis(chunk_decay, 1, 0)))
    prev = jnp.moveaxis(prev, 0, 1)
    y_off = jnp.einsum("bcign,bcgrpn->bcigrp", cm, prev) * jnp.exp(a_cum)[..., None]
    return (y_diag + y_off).reshape(bsz, L, g, r, p)


def ssd_mixer(z, xbc, dt_raw, conv_w, conv_b, dt_bias, a_log, d_skip, norm_w):
    bsz, L, _ = z.shape
    f32 = jnp.float32
    r = SSD_HEADS // SSD_GROUPS
    xbc = jax.nn.silu(causal_depthwise_conv(xbc, conv_w, conv_b))
    xs, bm, cm = jnp.split(xbc, [SSD_D_INNER, SSD_D_INNER + SSD_GROUPS * SSD_STATE], axis=-1)
    xs = xs.reshape(bsz, L, SSD_GROUPS, r, SSD_HEAD_DIM).astype(f32)
    bm = bm.reshape(bsz, L, SSD_GROUPS, SSD_STATE).astype(f32)
    cm = cm.reshape(bsz, L, SSD_GROUPS, SSD_STATE).astype(f32)
    dt = jax.nn.softplus(dt_raw.astype(f32) + dt_bias.astype(f32)).reshape(bsz, L, SSD_GROUPS, r)
    a = -jnp.exp(a_log.astype(f32)).reshape(SSD_GROUPS, r)
    y = ssd_chunked_scan(xs, dt, a, bm, cm)
    y = y + xs * d_skip.astype(f32).reshape(SSD_GROUPS, r)[..., None]
    y = y.reshape(bsz, L, SSD_D_INNER) * jax.nn.silu(z.astype(f32))
    return rms_norm(y, norm_w).astype(z.dtype)


def stick_breaking_attention(q, k, v):
    bsz, L, H, dh = q.shape
    scale = dh ** -0.5
    outs = []
    for s0 in range(0, L, Q_BLOCK):
        e0 = s0 + Q_BLOCK
        z = jnp.einsum("bthd,bshd->bhts", q[:, s0:e0], k[:, :e0],
                       preferred_element_type=jnp.float32) * scale
        t_idx = s0 + jnp.arange(Q_BLOCK)
        strict = jnp.arange(e0)[None, :] < t_idx[:, None]
        log_not = jnp.where(strict, jax.nn.log_sigmoid(-z), 0.0)
        after = lax.cumsum(log_not, axis=3, reverse=True) - log_not
        att = jnp.where(strict, jnp.exp(jax.nn.log_sigmoid(z) + after), 0.0)
        outs.append(jnp.einsum("bhts,bshd->bthd", att.astype(v.dtype), v[:, :e0]))
    return jnp.concatenate(outs, axis=1)


def dsa_attention(q, k, v, q_idx, k_idx, w_idx):
    bsz, L, H, dh = q.shape
    G = k.shape[2]
    topk = min(DSA_MAX_TOPK, L // 4)
    scale = dh ** -0.5
    outs = []
    for s0 in range(0, L, Q_BLOCK):
        e0 = s0 + Q_BLOCK
        t_idx = s0 + jnp.arange(Q_BLOCK)
        causal = jnp.arange(e0)[None, :] <= t_idx[:, None]
        logits = jnp.einsum("bthd,bsd->bths", q_idx[:, s0:e0], k_idx[:, :e0],
                            preferred_element_type=jnp.float32) * (IDX_DIM ** -0.5)
        score = jnp.einsum("bths,bth->bts", jax.nn.relu(logits),
                           w_idx[:, s0:e0].astype(jnp.float32) * (IDX_HEADS ** -0.5))
        score = jnp.where(causal, score, -jnp.inf)
        kk = min(topk, e0)
        _, sel = lax.top_k(score, kk)
        valid = sel <= t_idx[None, :, None]
        k_sel = jax.vmap(lambda kb, ib: kb[ib])(k, sel)
        v_sel = jax.vmap(lambda vb, ib: vb[ib])(v, sel)
        qb = q[:, s0:e0].reshape(bsz, Q_BLOCK, G, H // G, dh)
        s = jnp.einsum("btgrd,btkgd->btgrk", qb, k_sel, preferred_element_type=jnp.float32) * scale
        s = jnp.where(valid[:, :, None, None, :], s, -jnp.inf)
        p = jax.nn.softmax(s, axis=-1)
        o = jnp.einsum("btgrk,btkgd->btgrd", p.astype(v.dtype), v_sel)
        outs.append(o.reshape(bsz, Q_BLOCK, H, dh))
    return jnp.concatenate(outs, axis=1)


def setup_inputs(seed: int = 0) -> dict:
    key = jax.random.key(seed)
    ks = jax.random.split(key, 24)
    f32 = jnp.float32

    def nrm(k, shape, scale):
        return jax.random.normal(k, shape, f32) * scale

    dt = jnp.exp(jax.random.uniform(ks[10], (DEPTH, SSD_HEADS), f32)
                 * (math.log(0.1) - math.log(0.001)) + math.log(0.001))
    return {
        "x": nrm(ks[0], (BATCH, SEQ, D_MODEL), 1.0),
        "c": nrm(ks[1], (BATCH, D_MODEL), 1.0),
        "norm1_w": 1.0 + nrm(ks[2], (DEPTH, D_MODEL), 0.05),
        "ada_w": nrm(ks[3], (DEPTH, D_MODEL, 6 * D_MODEL), 0.5 * D_MODEL ** -0.5),
        "ada_b": nrm(ks[4], (DEPTH, 6 * D_MODEL), 0.02),
        "w_in": nrm(ks[5], (DEPTH, D_MODEL, IN_WIDTH), D_MODEL ** -0.5),
        "conv_w": nrm(ks[6], (DEPTH, SSD_CONV, SSD_CONV_CH), SSD_CONV ** -0.5),
        "conv_b": nrm(ks[7], (DEPTH, SSD_CONV_CH), 0.02),
        "dt_bias": dt + jnp.log(-jnp.expm1(-dt)),
        "a_log": jnp.log(jax.random.uniform(ks[8], (DEPTH, SSD_HEADS), f32, 1.0, 16.0)),
        "d_skip": 1.0 + nrm(ks[9], (DEPTH, SSD_HEADS), 0.1),
        "ssd_norm_w": 1.0 + nrm(ks[11], (DEPTH, SSD_D_INNER), 0.05),
        "w_br_ssd": nrm(ks[12], (DEPTH, SSD_D_INNER, D_MODEL), SSD_D_INNER ** -0.5),
        "w_br_sb": nrm(ks[13], (DEPTH, SB_WIDTH, D_MODEL), SB_WIDTH ** -0.5),
        "w_br_dsa": nrm(ks[14], (DEPTH, DSA_WIDTH, D_MODEL), DSA_WIDTH ** -0.5),
        "w_out": nrm(ks[15], (DEPTH, D_MODEL, D_MODEL), D_MODEL ** -0.5),
        "norm2_w": 1.0 + nrm(ks[16], (DEPTH, D_MODEL), 0.05),
        "w_up": nrm(ks[17], (DEPTH, D_MODEL, MLP_HIDDEN), D_MODEL ** -0.5),
        "w_down": nrm(ks[18], (DEPTH, MLP_HIDDEN, D_MODEL), MLP_HIDDEN ** -0.5),
        "final_norm_w": 1.0 + nrm(ks[19], (D_MODEL,), 0.05),
    }


def reference(x, c, norm1_w, ada_w, ada_b, w_in, conv_w, conv_b, dt_bias, a_log, d_skip,
              ssd_norm_w, w_br_ssd, w_br_sb, w_br_dsa, w_out, norm2_w, w_up, w_down,
              final_norm_w):
    bsz, L, _ = x.shape
    pos = jnp.arange(L, dtype=jnp.int32)
    split_at = np.cumsum(IN_SPLITS)[:-1].tolist()
    for l in range(DEPTH):
        mod = jax.nn.silu(c) @ ada_w[l] + ada_b[l]
        sh1, sc1, g1, sh2, sc2, g2 = jnp.split(mod[:, None, :], 6, axis=-1)

        h = rms_norm(x, norm1_w[l]) * (1.0 + sc1) + sh1
        (z, xbc, dt_raw, sb_q, sb_k, sb_v, ds_q, ds_k, ds_v,
         ix_q, ix_k, ix_w, gate_logits) = jnp.split(h @ w_in[l], split_at, axis=-1)

        y_ssd = ssd_mixer(z, xbc, dt_raw, conv_w[l], conv_b[l], dt_bias[l], a_log[l],
                          d_skip[l], ssd_norm_w[l])
        y_sb = stick_breaking_attention(
            sb_q.reshape(bsz, L, SB_HEADS, HEAD_DIM),
            sb_k.reshape(bsz, L, SB_HEADS, HEAD_DIM),
            sb_v.reshape(bsz, L, SB_HEADS, HEAD_DIM)).reshape(bsz, L, SB_WIDTH)
        y_dsa = dsa_attention(
            partial_rope(ds_q.reshape(bsz, L, DSA_HEADS, HEAD_DIM), pos),
            partial_rope(ds_k.reshape(bsz, L, DSA_KV_HEADS, HEAD_DIM), pos),
            ds_v.reshape(bsz, L, DSA_KV_HEADS, HEAD_DIM),
            partial_rope(ix_q.reshape(bsz, L, IDX_HEADS, IDX_DIM), pos),
            partial_rope(ix_k[:, :, None, :], pos)[:, :, 0],
            ix_w).reshape(bsz, L, DSA_WIDTH)

        g_ssd, g_sb, g_dsa = jnp.split(jax.nn.sigmoid(gate_logits), N_BRANCH, axis=-1)
        merged = (g_ssd * (y_ssd @ w_br_ssd[l]) + g_sb * (y_sb @ w_br_sb[l])
                  + g_dsa * (y_dsa @ w_br_dsa[l]))
        x = x + g1 * (merged @ w_out[l])

        h2 = rms_norm(x, norm2_w[l]) * (1.0 + sc2) + sh2
        x = x + g2 * (jnp.square(jax.nn.relu(h2 @ w_up[l])) @ w_down[l])
    return rms_norm(x, final_norm_w)
```

```python
import functools
import math

import jax
import jax.numpy as jnp
import numpy as np
from jax import lax
from jax.experimental import pallas as pl
from jax.experimental.pallas import tpu as pltpu

F32 = jnp.float32
BF16 = jnp.bfloat16
I32 = jnp.int32

NORM_EPS = 1e-6
HEAD_DIM = 64
ROPE_THETA = 500000.0
LANES = 128

SSD_HEADS = 16
SSD_GROUPS = 2
SSD_STATE = 128
SSD_CHUNK = 128
SSD_HEAD_DIM = 64
SSD_CONV = 4

SB_HEADS = 8
DSA_HEADS = 8
DSA_KV_HEADS = 2
IDX_HEADS = 8
IDX_DIM = 64
DSA_MAX_TOPK = 256

ATT_BLOCK = 256
SB_Q_BLOCK = 512
VMEM_LIMIT = 56 * 1024 * 1024

INT_MIN = -(2 ** 31)
NEG_BIG = -1e30


def _cparams(sem):
    return pltpu.CompilerParams(dimension_semantics=sem, vmem_limit_bytes=VMEM_LIMIT)


def _dot(a, b):
    return jnp.dot(a, b, preferred_element_type=F32)


def _dot_nt(a, b):
    return lax.dot_general(a, b, (((1,), (1,)), ((), ())), preferred_element_type=F32)


def _softplus(x):
    return jnp.maximum(x, 0.0) + jnp.log1p(jnp.exp(-jnp.abs(x)))


def _norm_mod(x, w, scale, shift):
    y = x * lax.rsqrt(jnp.mean(x * x, axis=-1, keepdims=True) + NORM_EPS)
    return (y * w) * (1.0 + scale) + shift


def _ada_kernel(c_ref, w_ref, b_ref, o_ref):
    c = c_ref[...]
    a = (c * jax.nn.sigmoid(c)).astype(BF16)
    o_ref[0] = _dot(a, w_ref[0].astype(BF16)) + b_ref[0]


def ada_mod(c, ada_w, ada_b, tn=1536):
    depth, d, n = ada_w.shape
    bsz = c.shape[0]
    return pl.pallas_call(
        _ada_kernel,
        out_shape=jax.ShapeDtypeStruct((depth, bsz, n), F32),
        grid=(depth, n // tn),
        in_specs=[
            pl.BlockSpec((bsz, d), lambda l, j: (0, 0)),
            pl.BlockSpec((1, d, tn), lambda l, j: (l, 0, j)),
            pl.BlockSpec((1, 1, tn), lambda l, j: (l, 0, j)),
        ],
        out_specs=pl.BlockSpec((1, bsz, tn), lambda l, j: (l, 0, j)),
        compiler_params=_cparams(("parallel", "parallel")),
        name="ada_mod",
    )(c, ada_w, ada_b.reshape(depth, 1, n))


def _proj_kernel(*refs, splits, rope_chunks, has_rope):
    if has_rope:
        x_ref, mod_ref, nw_ref, w_ref, cos_ref, s1_ref, s2_ref = refs[:7]
        out_refs = refs[7:]
    else:
        x_ref, mod_ref, nw_ref, w_ref = refs[:4]
        out_refs = refs[4:]
    h = _norm_mod(x_ref[...], nw_ref[...], mod_ref[0, 1:2, :], mod_ref[0, 0:1, :]).astype(BF16)
    off = 0
    for o_ref, width in zip(out_refs, splits):
        acc = _dot(h, w_ref[:, off:off + width])
        if has_rope:
            cos, s1, s2 = cos_ref[...], s1_ref[...], s2_ref[...]
            pieces = []
            for c in range(width // LANES):
                xc = acc[:, c * LANES:(c + 1) * LANES]
                if (off // LANES + c) in rope_chunks:
                    xc = (xc * cos + pltpu.roll(xc, LANES - 8, axis=1) * s1
                          + pltpu.roll(xc, 8, axis=1) * s2)
                pieces.append(xc)
            acc = pieces[0] if len(pieces) == 1 else jnp.concatenate(pieces, axis=1)
        o_ref[...] = acc.astype(o_ref.dtype)
        off += width


def proj(x2d, mod6, norm_w, w, splits, dtypes, seq_len, rope=None, rope_chunks=(), tm=512):
    m, d = x2d.shape
    n = w.shape[1]
    assert sum(splits) == n and m % tm == 0 and seq_len % tm == 0
    tiles_per_seq = seq_len // tm
    in_specs = [
        pl.BlockSpec((tm, d), lambda i: (i, 0)),
        pl.BlockSpec((1, 6, d), lambda i: (i // tiles_per_seq, 0, 0)),
        pl.BlockSpec((1, d), lambda i: (0, 0)),
        pl.BlockSpec((d, n), lambda i: (0, 0)),
    ]
    args = [x2d, mod6, norm_w.reshape(1, d), w]
    if rope is not None:
        for t in rope:
            in_specs.append(pl.BlockSpec((tm, LANES), lambda i: (i % tiles_per_seq, 0)))
            args.append(t)
    kern = functools.partial(_proj_kernel, splits=tuple(splits), rope_chunks=tuple(rope_chunks),
                             has_rope=rope is not None)
    return pl.pallas_call(
        kern,
        out_shape=[jax.ShapeDtypeStruct((m, s), dt) for s, dt in zip(splits, dtypes)],
        grid=(m // tm,),
        in_specs=in_specs,
        out_specs=[pl.BlockSpec((tm, s), lambda i: (i, 0)) for s in splits],
        compiler_params=_cparams(("parallel",)),
        name="proj",
    )(*args)


def rope_tables(seq_len):
    rot = HEAD_DIM // 4
    half = rot // 2
    inv_freq = jnp.exp(jnp.arange(half, dtype=F32) * (-2.0 * math.log(ROPE_THETA) / rot))
    ang = jnp.arange(seq_len, dtype=jnp.int32).astype(F32)[:, None] * inv_freq[None, :]
    cos, sin = jnp.cos(ang), jnp.sin(ang)
    ones = jnp.ones((seq_len, HEAD_DIM - rot), F32)
    zeros = jnp.zeros((seq_len, HEAD_DIM - rot), F32)
    zh = jnp.zeros((seq_len, half), F32)
    c64 = jnp.concatenate([cos, cos, ones], axis=1)
    s1_64 = jnp.concatenate([-sin, zh, zeros], axis=1)
    s2_64 = jnp.concatenate([zh, sin, zeros], axis=1)
    tile = lambda a: jnp.concatenate([a, a], axis=1)
    return tile(c64), tile(s1_64), tile(s2_64)


def _expand_heads(a):
    q = a.shape[0]
    lane = lax.broadcasted_iota(I32, (q, LANES), 1)
    chunks = []
    for c in range(SSD_HEADS // 2):
        lo = jnp.broadcast_to(a[:, 2 * c:2 * c + 1], (q, LANES))
        hi = jnp.broadcast_to(a[:, 2 * c + 1:2 * c + 2], (q, LANES))
        chunks.append(jnp.where(lane < SSD_HEAD_DIM, lo, hi))
    return jnp.concatenate(chunks, axis=1)


def _ssd_kernel(z_ref, xbc_ref, dt_ref, cw_ref, cb_ref, dtb_ref, alog_ref, dskip_ref, nw_ref,
                o_ref, state_ref, tail_ref):
    q = SSD_CHUNK
    di = SSD_HEADS * SSD_HEAD_DIM
    gw = SSD_STATE
    hpg = SSD_HEADS // SSD_GROUPS

    @pl.when(pl.program_id(1) == 0)
    def _():
        state_ref[...] = jnp.zeros_like(state_ref)
        tail_ref[...] = jnp.zeros_like(tail_ref)

    cur = xbc_ref[0]
    row = lax.broadcasted_iota(I32, (q, 1), 0)
    tail = tail_ref[...]
    conv = cb_ref[...] + cw_ref[SSD_CONV - 1:SSD_CONV, :] * cur
    for j in range(1, SSD_CONV):
        rolled = pltpu.roll(cur, j, axis=0)
        patch = jnp.tile(pltpu.roll(tail, j, axis=0), (q // 8, 1))
        shifted = jnp.where(row < j, patch, rolled)
        conv = conv + cw_ref[SSD_CONV - 1 - j:SSD_CONV - j, :] * shifted
    tail_ref[...] = cur[q - 8:, :]
    xbc = conv * jax.nn.sigmoid(conv)
    xs = xbc[:, :di]
    bm = xbc[:, di:di + SSD_GROUPS * gw]
    cm = xbc[:, di + SSD_GROUPS * gw:]

    dt = _softplus(dt_ref[0] + dtb_ref[...])
    da = dt * (-jnp.exp(alog_ref[...]))
    acum = da
    s = 1
    while s < q:
        acum = acum + jnp.where(row >= s, pltpu.roll(acum, s, axis=0), 0.0)
        s *= 2
    acum_t = acum.T
    a_last = acum[q - 1:q, :]
    e_in = jnp.exp(acum)
    e_out = jnp.exp(a_last - acum)

    dt_x = _expand_heads(dt)
    e_in_x = _expand_heads(e_in)
    e_out_x = _expand_heads(e_out)
    e_last_x = _expand_heads(jnp.exp(a_last))

    xdt = xs * dt_x
    xdt_b = xdt.astype(BF16)
    xdt_out_b = (xdt * e_out_x).astype(BF16)

    ri = lax.broadcasted_iota(I32, (q, q), 0)
    ci = lax.broadcasted_iota(I32, (q, q), 1)
    causal = ri >= ci

    y_parts = []
    new_states = []
    for g in range(SSD_GROUPS):
        bm_g = bm[:, g * gw:(g + 1) * gw]
        cm_g = cm[:, g * gw:(g + 1) * gw].astype(BF16)
        bm_gb = bm_g.astype(BF16)
        cb = _dot_nt(cm_g, bm_gb)
        sl = slice(g * hpg * SSD_HEAD_DIM, (g + 1) * hpg * SSD_HEAD_DIM)
        st_g = state_ref[:, sl]
        y_off = _dot(cm_g, st_g.astype(BF16)) * e_in_x[:, sl]
        for r in range(hpg):
            h = g * hpg + r
            diff = acum[:, h:h + 1] - acum_t[h:h + 1, :]
            lmat = jnp.exp(jnp.where(causal, diff, NEG_BIG))
            m_h = (cb * lmat).astype(BF16)
            y_parts.append(_dot(m_h, xdt_b[:, h * SSD_HEAD_DIM:(h + 1) * SSD_HEAD_DIM])
                           + y_off[:, r * SSD_HEAD_DIM:(r + 1) * SSD_HEAD_DIM])
        upd = _dot(bm_g.T.astype(BF16), xdt_out_b[:, sl])
        new_states.append(st_g * e_last_x[:, sl] + upd)
    for g in range(SSD_GROUPS):
        sl = slice(g * hpg * SSD_HEAD_DIM, (g + 1) * hpg * SSD_HEAD_DIM)
        state_ref[:, sl] = new_states[g]

    y = jnp.concatenate(y_parts, axis=1) + xs * _expand_heads(dskip_ref[...])
    zg = z_ref[0]
    y = y * (zg * jax.nn.sigmoid(zg))
    y = y * lax.rsqrt(jnp.mean(y * y, axis=-1, keepdims=True) + NORM_EPS) * nw_ref[...]
    o_ref[0] = y.astype(o_ref.dtype)


def ssd_mixer(z, xbc, dt, conv_w, conv_b, dt_bias, a_log, d_skip, norm_w):
    bsz, seq, di = z.shape
    cch = xbc.shape[-1]
    nc = seq // SSD_CHUNK
    pad = lambda v: jnp.pad(v.astype(F32), (0, LANES - v.shape[0])).reshape(1, LANES)
    full = lambda shape: pl.BlockSpec(shape, lambda b, c: (0,) * len(shape))
    return pl.pallas_call(
        _ssd_kernel,
        out_shape=jax.ShapeDtypeStruct((bsz, seq, di), BF16),
        grid=(bsz, nc),
        in_specs=[
            pl.BlockSpec((1, SSD_CHUNK, di), lambda b, c: (b, c, 0)),
            pl.BlockSpec((1, SSD_CHUNK, cch), lambda b, c: (b, c, 0)),
            pl.BlockSpec((1, SSD_CHUNK, LANES), lambda b, c: (b, c, 0)),
            full((SSD_CONV, cch)), full((1, cch)),
            full((1, LANES)), full((1, LANES)), full((1, LANES)), full((1, di)),
        ],
        out_specs=pl.BlockSpec((1, SSD_CHUNK, di), lambda b, c: (b, c, 0)),
        scratch_shapes=[pltpu.VMEM((SSD_STATE, di), F32), pltpu.VMEM((8, cch), F32)],
        compiler_params=_cparams(("parallel", "arbitrary")),
        name="ssd",
    )(z, xbc, dt, conv_w, conv_b.reshape(1, cch), pad(dt_bias), pad(a_log), pad(d_skip),
      norm_w.reshape(1, di))


def _sb_kernel(q_ref, k_ref, v_ref, o_ref, acc_ref, carry_ref):
    tq, tk = SB_Q_BLOCK, ATT_BLOCK
    qi = pl.program_id(2)
    row = lax.broadcasted_iota(I32, (tq, tk), 0)
    col = lax.broadcasted_iota(I32, (tq, tk), 1)
    sr = lax.broadcasted_iota(I32, (2 * tk, tk + LANES), 0) & (tk - 1)
    sc = lax.broadcasted_iota(I32, (2 * tk, tk + LANES), 1)
    suffix = jnp.where((sr > sc) | (sc >= tk), 1.0, 0.0).astype(BF16)

    acc_ref[...] = jnp.zeros_like(acc_ref)
    carry_ref[...] = jnp.zeros_like(carry_ref)

    def block(j, masked):
        start = pl.multiple_of(j * tk, tk)
        if masked:
            strict = (j * tk + col) < (qi * tq + row)
        for hh in range(2):
            lanes = slice(hh * HEAD_DIM, (hh + 1) * HEAD_DIM)
            kj = k_ref[0, pl.ds(start, tk), lanes]
            vj = v_ref[0, pl.ds(start, tk), lanes]
            z = _dot_nt(q_ref[0, :, lanes], kj)
            sp = jnp.maximum(z, 0.0) + jnp.log(1.0 + jnp.exp(-jnp.abs(z)))
            spm = jnp.where(strict, sp, 0.0) if masked else sp
            hi = spm.astype(BF16)
            lo = (spm - hi.astype(F32)).astype(BF16)
            r = _dot(jnp.concatenate([hi, lo], axis=1), suffix)
            carry = carry_ref[hh]
            after = r[:, :tk] + jnp.concatenate([carry, carry], axis=1)
            att = jnp.exp(z - sp - after)
            if masked:
                att = jnp.where(strict, att, 0.0)
            acc_ref[hh] += _dot(att.astype(BF16), vj)
            carry_ref[hh] = carry + r[:, tk:]

    nkb = (qi + 1) * (tq // tk)
    for d in range(tq // tk):
        block(nkb - 1 - d, True)

    def body(i, c):
        block(nkb - tq // tk - 1 - i, False)
        return c

    lax.fori_loop(0, nkb - tq // tk, body, 0)
    o_ref[0] = jnp.concatenate([acc_ref[0], acc_ref[1]], axis=1).astype(o_ref.dtype)


def sb_attention(qkv, bsz, seq):
    width = SB_HEADS * HEAD_DIM
    npair = width // LANES
    nq = seq // SB_Q_BLOCK
    return pl.pallas_call(
        _sb_kernel,
        out_shape=jax.ShapeDtypeStruct((bsz, seq, width), BF16),
        grid=(bsz, npair, nq),
        in_specs=[
            pl.BlockSpec((1, SB_Q_BLOCK, LANES), lambda b, p, i: (b, i, p)),
            pl.BlockSpec((1, seq, LANES), lambda b, p, i: (b, 0, npair + p)),
            pl.BlockSpec((1, seq, LANES), lambda b, p, i: (b, 0, 2 * npair + p)),
        ],
        out_specs=pl.BlockSpec((1, SB_Q_BLOCK, LANES), lambda b, p, i: (b, i, p)),
        scratch_shapes=[pltpu.VMEM((2, SB_Q_BLOCK, HEAD_DIM), F32),
                        pltpu.VMEM((2, SB_Q_BLOCK, LANES), F32)],
        compiler_params=_cparams(("parallel", "parallel", "arbitrary")),
        name="stickbreak",
    )(qkv, qkv, qkv)


def _fold_rows(m):
    parts = [m[i * 8:(i + 1) * 8, :] for i in range(m.shape[0] // 8)]
    while len(parts) > 1:
        parts = [a + b for a, b in zip(parts[0::2], parts[1::2])]
    return parts[0]


def _ordered_to_float(u):
    key = u ^ INT_MIN
    bits = jnp.where(key < 0, key ^ 0x7FFFFFFF, key)
    return pltpu.bitcast(bits, F32)


def _dsa_kernel(dq_ref, dk_ref, dv_ref, iq_ref, ik_ref, iw_ref, o_ref,
                sc_ref, bias_ref, qs_ref, s_ref, mx_ref, l_ref, acc_ref):
    t = ATT_BLOCK
    qi = pl.program_id(1)
    nkb = qi + 1
    rpg = DSA_HEADS // DSA_KV_HEADS
    topk = float(DSA_MAX_TOPK)
    si = lax.broadcasted_iota(I32, (t, t), 0)
    ti = lax.broadcasted_iota(I32, (t, t), 1)

    for h in range(IDX_HEADS):
        qs_ref[h * t:(h + 1) * t, :] = iq_ref[0, :, h * IDX_DIM:(h + 1) * IDX_DIM]
    wt = (iw_ref[0] * (IDX_HEADS ** -0.5)).T

    def score_block(j, c):
        start = pl.multiple_of(j * t, t)
        ikj = ik_ref[0, pl.ds(start, t), 0:IDX_DIM]
        logits = _dot_nt(ikj, qs_ref[...])
        score = jnp.zeros((t, t), F32)
        for h in range(IDX_HEADS):
            score = score + wt[h:h + 1, :] * jnp.maximum(logits[:, h * t:(h + 1) * t], 0.0)
        causal = (j * t + si) <= (qi * t + ti)
        sc_ref[j] = jnp.where(causal, score, -jnp.inf)
        return c

    lax.fori_loop(0, nkb, score_block, 0)

    def count_ge(cand):
        def body(j, cnt):
            return cnt + _fold_rows(jnp.where(sc_ref[j] >= cand, 1.0, 0.0))
        cnt = lax.fori_loop(0, nkb, body, jnp.zeros((8, t), F32))
        return jnp.sum(cnt, axis=0, keepdims=True)

    @pl.when(qi == 0)
    def _():
        bias_ref[0] = jnp.where(ti <= si, 0.0, NEG_BIG)

    @pl.when(qi > 0)
    def _():
        def bit_step(i, c):
            thr_u, cnt_thr = c
            cand_u = thr_u | lax.shift_left(jnp.int32(1), 31 - i)
            cnt = count_ge(_ordered_to_float(cand_u))
            ok = cnt >= topk
            return jnp.where(ok, cand_u, thr_u), jnp.where(ok, cnt, cnt_thr)

        thr_u, cnt_thr = lax.fori_loop(
            0, 32, bit_step, (jnp.zeros((1, t), I32), jnp.full((1, t), topk, F32)))
        thr = _ordered_to_float(thr_u)
        excess = jnp.max(cnt_thr) > topk

        @pl.when(jnp.logical_not(excess))
        def _():
            def fill(j, c):
                bias_ref[j] = jnp.where(sc_ref[j] >= thr, 0.0, NEG_BIG).T
                return c
            lax.fori_loop(0, nkb, fill, 0)

        @pl.when(excess)
        def _():
            nxt = _ordered_to_float(thr_u + 1)
            need = topk - count_ge(nxt)
            incl = jnp.where(ti <= si, 1.0, 0.0).astype(BF16)

            def fix(j, seen):
                sc = sc_ref[j]
                gt = sc >= nxt
                eq = jnp.logical_and(sc >= thr, jnp.logical_not(gt))
                rank = _dot(incl, jnp.where(eq, 1.0, 0.0).astype(BF16)) + seen
                sel = jnp.logical_or(gt, jnp.logical_and(eq, rank <= need))
                bias_ref[j] = jnp.where(sel, 0.0, NEG_BIG).T
                return rank[t - 1:t, :]

            lax.fori_loop(0, nkb, fix, jnp.zeros((1, t), F32))

    for g in range(DSA_KV_HEADS):
        for r in range(rpg):
            h = g * rpg + r
            qs_ref[r * t:(r + 1) * t, :] = dq_ref[0, :, h * HEAD_DIM:(h + 1) * HEAD_DIM]
        lanes = slice(g * HEAD_DIM, (g + 1) * HEAD_DIM)
        mx_ref[...] = jnp.full(mx_ref.shape, NEG_BIG, F32)

        def pass_scores(j, c):
            start = pl.multiple_of(j * t, t)
            kj = dk_ref[0, pl.ds(start, t), lanes]
            s = _dot_nt(qs_ref[0:rpg * t, :], kj) + jnp.concatenate([bias_ref[j]] * rpg, axis=0)
            s_ref[j] = s
            mx_ref[...] = jnp.maximum(mx_ref[...], jnp.maximum(s[:, :LANES], s[:, LANES:]))
            return c

        lax.fori_loop(0, nkb, pass_scores, 0)
        m = jnp.max(mx_ref[...], axis=1, keepdims=True)
        mx_ref[...] = jnp.broadcast_to(m, mx_ref.shape)
        l_ref[...] = jnp.zeros_like(l_ref)
        acc_ref[...] = jnp.zeros_like(acc_ref)

        def pass_values(j, c):
            start = pl.multiple_of(j * t, t)
            vj = dv_ref[0, pl.ds(start, t), lanes]
            mb = mx_ref[...]
            p = jnp.exp(s_ref[j] - jnp.concatenate([mb, mb], axis=1))
            l_ref[...] += p[:, :LANES] + p[:, LANES:]
            acc_ref[...] += _dot(p.astype(BF16), vj)
            return c

        lax.fori_loop(0, nkb, pass_values, 0)
        out = acc_ref[...] / jnp.sum(l_ref[...], axis=1, keepdims=True)
        for r in range(rpg):
            h = g * rpg + r
            o_ref[0, :, h * HEAD_DIM:(h + 1) * HEAD_DIM] = out[r * t:(r + 1) * t, :].astype(o_ref.dtype)


def dsa_attention(dq, dk, dv, iq, ik, iw, bsz, seq):
    t = ATT_BLOCK
    nq = seq // t
    rpg = DSA_HEADS // DSA_KV_HEADS
    qblk = lambda w: pl.BlockSpec((1, t, w), lambda b, i: (b, i, 0))
    kblk = pl.BlockSpec((1, seq, LANES), lambda b, i: (b, 0, 0))
    return pl.pallas_call(
        _dsa_kernel,
        out_shape=jax.ShapeDtypeStruct((bsz, seq, DSA_HEADS * HEAD_DIM), BF16),
        grid=(bsz, nq),
        in_specs=[qblk(DSA_HEADS * HEAD_DIM), kblk, kblk, qblk(IDX_HEADS * IDX_DIM), kblk, qblk(LANES)],
        out_specs=qblk(DSA_HEADS * HEAD_DIM),
        scratch_shapes=[
            pltpu.VMEM((nq, t, t), F32),
            pltpu.VMEM((nq, t, t), F32),
            pltpu.VMEM((IDX_HEADS * t, HEAD_DIM), BF16),
            pltpu.VMEM((nq, rpg * t, t), F32),
            pltpu.VMEM((rpg * t, LANES), F32),
            pltpu.VMEM((rpg * t, LANES), F32),
            pltpu.VMEM((rpg * t, HEAD_DIM), F32),
        ],
        compiler_params=_cparams(("parallel", "arbitrary")),
        name="dsa",
    )(dq, dk, dv, iq, ik, iw)


def _merge_kernel(x_ref, mod_ref, ys_ref, yb_ref, yd_ref, g0_ref, g1_ref, g2_ref,
                  ws_ref, wb_ref, wd_ref, wo_ref, o_ref):
    merged = (jax.nn.sigmoid(g0_ref[...]) * _dot(ys_ref[...], ws_ref[...])
              + jax.nn.sigmoid(g1_ref[...]) * _dot(yb_ref[...], wb_ref[...])
              + jax.nn.sigmoid(g2_ref[...]) * _dot(yd_ref[...], wd_ref[...]))
    o_ref[...] = x_ref[...] + mod_ref[0, 2:3, :] * _dot(merged.astype(BF16), wo_ref[...])


def merge(x2d, mod6, y_ssd, y_sb, y_dsa, gates, w_ssd, w_sb, w_dsa, w_out, seq_len, tm=512):
    m, d = x2d.shape
    tiles_per_seq = seq_len // tm
    row = lambda w: pl.BlockSpec((tm, w), lambda i: (i, 0))
    full = lambda a: pl.BlockSpec(a.shape, lambda i: (0, 0))
    return pl.pallas_call(
        _merge_kernel,
        out_shape=jax.ShapeDtypeStruct((m, d), F32),
        grid=(m // tm,),
        in_specs=[
            row(d),
            pl.BlockSpec((1, 6, d), lambda i: (i // tiles_per_seq, 0, 0)),
            row(y_ssd.shape[1]), row(y_sb.shape[1]), row(y_dsa.shape[1]),
            pl.BlockSpec((tm, d), lambda i: (i, 0)),
            pl.BlockSpec((tm, d), lambda i: (i, 1)),
            pl.BlockSpec((tm, d), lambda i: (i, 2)),
            full(w_ssd), full(w_sb), full(w_dsa), full(w_out),
        ],
        out_specs=row(d),
        compiler_params=_cparams(("parallel",)),
        name="merge",
    )(x2d, mod6, y_ssd, y_sb, y_dsa, gates, gates, gates, w_ssd, w_sb, w_dsa, w_out)


def _mlp_kernel(x_ref, mod_ref, nw_ref, wu_ref, wd_ref, fw_ref, o_ref, h_ref, acc_ref, *, final):
    j = pl.program_id(1)

    @pl.when(j == 0)
    def _():
        h_ref[...] = _norm_mod(x_ref[...], nw_ref[...], mod_ref[0, 4:5, :],
                               mod_ref[0, 3:4, :]).astype(BF16)
        acc_ref[...] = jnp.zeros_like(acc_ref)

    u = jnp.maximum(_dot(h_ref[...], wu_ref[...]), 0.0)
    acc_ref[...] += _dot((u * u).astype(BF16), wd_ref[...])

    @pl.when(j == pl.num_programs(1) - 1)
    def _():
        y = x_ref[...] + mod_ref[0, 5:6, :] * acc_ref[...]
        if final:
            y = y * lax.rsqrt(jnp.mean(y * y, axis=-1, keepdims=True) + NORM_EPS) * fw_ref[...]
        o_ref[...] = y


def mlp(x2d, mod6, norm_w, w_up, w_down, final_w, seq_len, final, tm=1024, th=512):
    m, d = x2d.shape
    hid = w_up.shape[1]
    tiles_per_seq = seq_len // tm
    return pl.pallas_call(
        functools.partial(_mlp_kernel, final=final),
        out_shape=jax.ShapeDtypeStruct((m, d), F32),
        grid=(m // tm, hid // th),
        in_specs=[
            pl.BlockSpec((tm, d), lambda i, j: (i, 0)),
            pl.BlockSpec((1, 6, d), lambda i, j: (i // tiles_per_seq, 0, 0)),
            pl.BlockSpec((1, d), lambda i, j: (0, 0)),
            pl.BlockSpec((d, th), lambda i, j: (0, j)),
            pl.BlockSpec((th, d), lambda i, j: (j, 0)),
            pl.BlockSpec((1, d), lambda i, j: (0, 0)),
        ],
        out_specs=pl.BlockSpec((tm, d), lambda i, j: (i, 0)),
        scratch_shapes=[pltpu.VMEM((tm, d), BF16), pltpu.VMEM((tm, d), F32)],
        compiler_params=_cparams(("parallel", "arbitrary")),
        name="mlp",
    )(x2d, mod6, norm_w.reshape(1, d), w_up, w_down, final_w.reshape(1, d))


def _split_w_in(w):
    d = w.shape[0]
    di = SSD_HEADS * SSD_HEAD_DIM
    cch = di + 2 * SSD_GROUPS * SSD_STATE
    sbw = SB_HEADS * HEAD_DIM
    dsw = DSA_HEADS * HEAD_DIM
    kvw = DSA_KV_HEADS * HEAD_DIM
    sizes = [di, cch, SSD_HEADS, sbw, sbw, sbw, dsw, kvw, kvw, IDX_HEADS * IDX_DIM, IDX_DIM, IDX_HEADS]
    offs = np.cumsum([0] + sizes)
    seg = [w[:, offs[i]:offs[i + 1]] for i in range(len(sizes))]
    gate = w[:, offs[-1]:]
    z_w, xbc_w, dt_w, sq, sk, sv, dq, dk, dv, iq, ik, iw = seg
    zpad = lambda a, n: jnp.pad(a, ((0, 0), (0, n - a.shape[1])))
    scale = HEAD_DIM ** -0.5
    w_ssd = jnp.concatenate([z_w, xbc_w, zpad(dt_w, LANES)], axis=1)
    w_sb = jnp.concatenate([sq * scale, sk, sv], axis=1)
    w_dsa = jnp.concatenate([dq * scale, dk, dv, iq * (IDX_DIM ** -0.5), zpad(ik, LANES),
                             zpad(iw, LANES)], axis=1)
    return (w_ssd.astype(BF16), w_sb.astype(BF16), w_dsa.astype(BF16), gate.astype(BF16))


def kernel(x, c, norm1_w, ada_w, ada_b, w_in, conv_w, conv_b, dt_bias, a_log, d_skip, ssd_norm_w,
           w_br_ssd, w_br_sb, w_br_dsa, w_out, norm2_w, w_up, w_down, final_norm_w):
    bsz, seq, d = x.shape
    depth = ada_w.shape[0]
    m = bsz * seq
    di = SSD_HEADS * SSD_HEAD_DIM
    cch = di + 2 * SSD_GROUPS * SSD_STATE
    dsw = DSA_HEADS * HEAD_DIM
    rope = rope_tables(seq)
    mods = ada_mod(c, ada_w, ada_b).reshape(depth, bsz, 6, d)
    x2d = x.reshape(m, d)
    for l in range(depth):
        mod6 = mods[l]
        w_ssd, w_sb, w_dsa, w_gate = _split_w_in(w_in[l])
        z, xbc, dt = proj(x2d, mod6, norm1_w[l], w_ssd, (di, cch, LANES), (F32, F32, F32), seq)
        (qkv,) = proj(x2d, mod6, norm1_w[l], w_sb, (3 * SB_HEADS * HEAD_DIM,), (BF16,), seq)
        dq, dk, dv, iq, ik, iw = proj(
            x2d, mod6, norm1_w[l], w_dsa, (dsw, LANES, LANES, dsw, LANES, LANES),
            (BF16, BF16, BF16, BF16, BF16, F32), seq, rope=rope, rope_chunks=(0, 1, 2, 3, 4, 6, 7, 8, 9, 10))
        (gates,) = proj(x2d, mod6, norm1_w[l], w_gate, (3 * d,), (F32,), seq)

        r3 = lambda a: a.reshape(bsz, seq, a.shape[-1])
        y_ssd = ssd_mixer(r3(z), r3(xbc), r3(dt), conv_w[l], conv_b[l], dt_bias[l], a_log[l],
                          d_skip[l], ssd_norm_w[l])
        y_sb = sb_attention(r3(qkv), bsz, seq)
        y_dsa = dsa_attention(r3(dq), r3(dk), r3(dv), r3(iq), r3(ik), r3(iw), bsz, seq)

        x2d = merge(x2d, mod6, y_ssd.reshape(m, di), y_sb.reshape(m, -1), y_dsa.reshape(m, -1), gates,
                    w_br_ssd[l].astype(BF16), w_br_sb[l].astype(BF16), w_br_dsa[l].astype(BF16),
                    w_out[l].astype(BF16), seq)
        x2d = mlp(x2d, mod6, norm2_w[l], w_up[l].astype(BF16), w_down[l].astype(BF16), final_norm_w,
                  seq, final=(l == depth - 1))
    return x2d.reshape(bsz, seq, d)
```

```python
import functools
import math

import jax
import jax.numpy as jnp
import numpy as np
from jax import lax
from jax.experimental import pallas as pl
from jax.experimental.pallas import tpu as pltpu

F32 = jnp.float32
BF16 = jnp.bfloat16
I32 = jnp.int32

NORM_EPS = 1e-6
HEAD_DIM = 64
ROPE_THETA = 500000.0
LANES = 128

SSD_HEADS = 16
SSD_GROUPS = 2
SSD_STATE = 128
SSD_CHUNK = 128
SSD_HEAD_DIM = 64
SSD_CONV = 4

SB_HEADS = 8
DSA_HEADS = 8
DSA_KV_HEADS = 2
IDX_HEADS = 8
IDX_DIM = 64
DSA_MAX_TOPK = 256

ATT_BLOCK = 256
SB_Q_BLOCK = 512
VMEM_LIMIT = 56 * 1024 * 1024

LOG2E = 1.4426950408889634
INT_MIN = -(2 ** 31)
NEG_BIG = -1e30


def _cparams(sem):
    return pltpu.CompilerParams(dimension_semantics=sem, vmem_limit_bytes=VMEM_LIMIT)


def _dot(a, b):
    return jnp.dot(a, b, preferred_element_type=F32)


def _dot_nt(a, b):
    return lax.dot_general(a, b, (((1,), (1,)), ((), ())), preferred_element_type=F32)


def _softplus(x):
    return jnp.maximum(x, 0.0) + jnp.log1p(jnp.exp(-jnp.abs(x)))


def _norm_mod(x, w, scale, shift):
    y = x * lax.rsqrt(jnp.mean(x * x, axis=-1, keepdims=True) + NORM_EPS)
    return (y * w) * (1.0 + scale) + shift


def _ada_kernel(c_ref, w_ref, b_ref, o_ref):
    c = c_ref[...]
    a = (c * jax.nn.sigmoid(c)).astype(BF16)
    o_ref[0] = _dot(a, w_ref[0].astype(BF16)) + b_ref[0]


def ada_mod(c, ada_w, ada_b, tn=1536):
    depth, d, n = ada_w.shape
    bsz = c.shape[0]
    return pl.pallas_call(
        _ada_kernel,
        out_shape=jax.ShapeDtypeStruct((depth, bsz, n), F32),
        grid=(depth, n // tn),
        in_specs=[
            pl.BlockSpec((bsz, d), lambda l, j: (0, 0)),
            pl.BlockSpec((1, d, tn), lambda l, j: (l, 0, j)),
            pl.BlockSpec((1, 1, tn), lambda l, j: (l, 0, j)),
        ],
        out_specs=pl.BlockSpec((1, bsz, tn), lambda l, j: (l, 0, j)),
        compiler_params=_cparams(("parallel", "parallel")),
        name="ada_mod",
    )(c, ada_w, ada_b.reshape(depth, 1, n))


def _proj_kernel(*refs, splits, rope_chunks, has_rope):
    if has_rope:
        x_ref, mod_ref, nw_ref, w_ref, cos_ref, s1_ref, s2_ref = refs[:7]
        out_refs = refs[7:]
    else:
        x_ref, mod_ref, nw_ref, w_ref = refs[:4]
        out_refs = refs[4:]
    h = _norm_mod(x_ref[...], nw_ref[...], mod_ref[0, 1:2, :], mod_ref[0, 0:1, :]).astype(BF16)
    off = 0
    for o_ref, width in zip(out_refs, splits):
        acc = _dot(h, w_ref[:, off:off + width])
        if has_rope:
            cos, s1, s2 = cos_ref[...], s1_ref[...], s2_ref[...]
            pieces = []
            for c in range(width // LANES):
                xc = acc[:, c * LANES:(c + 1) * LANES]
                if (off // LANES + c) in rope_chunks:
                    xc = (xc * cos + pltpu.roll(xc, LANES - 8, axis=1) * s1
                          + pltpu.roll(xc, 8, axis=1) * s2)
                pieces.append(xc)
            acc = pieces[0] if len(pieces) == 1 else jnp.concatenate(pieces, axis=1)
        o_ref[...] = acc.astype(o_ref.dtype)
        off += width


def proj(x2d, mod6, norm_w, w, splits, dtypes, seq_len, rope=None, rope_chunks=(), tm=512):
    m, d = x2d.shape
    n = w.shape[1]
    assert sum(splits) == n and m % tm == 0 and seq_len % tm == 0
    tiles_per_seq = seq_len // tm
    in_specs = [
        pl.BlockSpec((tm, d), lambda i: (i, 0)),
        pl.BlockSpec((1, 6, d), lambda i: (i // tiles_per_seq, 0, 0)),
        pl.BlockSpec((1, d), lambda i: (0, 0)),
        pl.BlockSpec((d, n), lambda i: (0, 0)),
    ]
    args = [x2d, mod6, norm_w.reshape(1, d), w]
    if rope is not None:
        for t in rope:
            in_specs.append(pl.BlockSpec((tm, LANES), lambda i: (i % tiles_per_seq, 0)))
            args.append(t)
    kern = functools.partial(_proj_kernel, splits=tuple(splits), rope_chunks=tuple(rope_chunks),
                             has_rope=rope is not None)
    return pl.pallas_call(
        kern,
        out_shape=[jax.ShapeDtypeStruct((m, s), dt) for s, dt in zip(splits, dtypes)],
        grid=(m // tm,),
        in_specs=in_specs,
        out_specs=[pl.BlockSpec((tm, s), lambda i: (i, 0)) for s in splits],
        compiler_params=_cparams(("parallel",)),
        name="proj",
    )(*args)


def rope_tables(seq_len):
    rot = HEAD_DIM // 4
    half = rot // 2
    inv_freq = jnp.exp(jnp.arange(half, dtype=F32) * (-2.0 * math.log(ROPE_THETA) / rot))
    ang = jnp.arange(seq_len, dtype=jnp.int32).astype(F32)[:, None] * inv_freq[None, :]
    cos, sin = jnp.cos(ang), jnp.sin(ang)
    ones = jnp.ones((seq_len, HEAD_DIM - rot), F32)
    zeros = jnp.zeros((seq_len, HEAD_DIM - rot), F32)
    zh = jnp.zeros((seq_len, half), F32)
    c64 = jnp.concatenate([cos, cos, ones], axis=1)
    s1_64 = jnp.concatenate([-sin, zh, zeros], axis=1)
    s2_64 = jnp.concatenate([zh, sin, zeros], axis=1)
    tile = lambda a: jnp.concatenate([a, a], axis=1)
    return tile(c64), tile(s1_64), tile(s2_64)


def _expand_heads(a):
    q = a.shape[0]
    lane = lax.broadcasted_iota(I32, (q, LANES), 1)
    chunks = []
    for c in range(SSD_HEADS // 2):
        lo = jnp.broadcast_to(a[:, 2 * c:2 * c + 1], (q, LANES))
        hi = jnp.broadcast_to(a[:, 2 * c + 1:2 * c + 2], (q, LANES))
        chunks.append(jnp.where(lane < SSD_HEAD_DIM, lo, hi))
    return jnp.concatenate(chunks, axis=1)


def _ssd_kernel(z_ref, xbc_ref, dt_ref, cw_ref, cb_ref, dtb_ref, alog_ref, dskip_ref, nw_ref,
                o_ref, state_ref, tail_ref):
    q = SSD_CHUNK
    di = SSD_HEADS * SSD_HEAD_DIM
    gw = SSD_STATE
    hpg = SSD_HEADS // SSD_GROUPS

    @pl.when(pl.program_id(1) == 0)
    def _():
        state_ref[...] = jnp.zeros_like(state_ref)
        tail_ref[...] = jnp.zeros_like(tail_ref)

    cur = xbc_ref[0]
    row = lax.broadcasted_iota(I32, (q, 1), 0)
    tail = tail_ref[...]
    conv = cb_ref[...] + cw_ref[SSD_CONV - 1:SSD_CONV, :] * cur
    for j in range(1, SSD_CONV):
        rolled = pltpu.roll(cur, j, axis=0)
        patch = jnp.tile(pltpu.roll(tail, j, axis=0), (q // 8, 1))
        shifted = jnp.where(row < j, patch, rolled)
        conv = conv + cw_ref[SSD_CONV - 1 - j:SSD_CONV - j, :] * shifted
    tail_ref[...] = cur[q - 8:, :]
    xbc = conv * jax.nn.sigmoid(conv)
    xs = xbc[:, :di]
    bm = xbc[:, di:di + SSD_GROUPS * gw]
    cm = xbc[:, di + SSD_GROUPS * gw:]

    dt = _softplus(dt_ref[0] + dtb_ref[...])
    da = dt * (-jnp.exp(alog_ref[...]))
    acum = da
    s = 1
    while s < q:
        acum = acum + jnp.where(row >= s, pltpu.roll(acum, s, axis=0), 0.0)
        s *= 2
    acum_t = acum.T
    a_last = acum[q - 1:q, :]
    e_in = jnp.exp(acum)
    e_out = jnp.exp(a_last - acum)

    dt_x = _expand_heads(dt)
    e_in_x = _expand_heads(e_in)
    e_out_x = _expand_heads(e_out)
    e_last_x = _expand_heads(jnp.exp(a_last))

    xdt = xs * dt_x
    xdt_b = xdt.astype(BF16)
    xdt_out_b = (xdt * e_out_x).astype(BF16)

    ri = lax.broadcasted_iota(I32, (q, q), 0)
    ci = lax.broadcasted_iota(I32, (q, q), 1)
    causal = ri >= ci

    y_parts = []
    new_states = []
    for g in range(SSD_GROUPS):
        bm_g = bm[:, g * gw:(g + 1) * gw]
        cm_g = cm[:, g * gw:(g + 1) * gw].astype(BF16)
        bm_gb = bm_g.astype(BF16)
        cb = _dot_nt(cm_g, bm_gb)
        sl = slice(g * hpg * SSD_HEAD_DIM, (g + 1) * hpg * SSD_HEAD_DIM)
        st_g = state_ref[:, sl]
        y_off = _dot(cm_g, st_g.astype(BF16)) * e_in_x[:, sl]
        for r in range(hpg):
            h = g * hpg + r
            diff = acum[:, h:h + 1] - acum_t[h:h + 1, :]
            lmat = jnp.exp(jnp.where(causal, diff, NEG_BIG))
            m_h = (cb * lmat).astype(BF16)
            y_parts.append(_dot(m_h, xdt_b[:, h * SSD_HEAD_DIM:(h + 1) * SSD_HEAD_DIM])
                           + y_off[:, r * SSD_HEAD_DIM:(r + 1) * SSD_HEAD_DIM])
        upd = _dot(bm_g.T.astype(BF16), xdt_out_b[:, sl])
        new_states.append(st_g * e_last_x[:, sl] + upd)
    for g in range(SSD_GROUPS):
        sl = slice(g * hpg * SSD_HEAD_DIM, (g + 1) * hpg * SSD_HEAD_DIM)
        state_ref[:, sl] = new_states[g]

    y = jnp.concatenate(y_parts, axis=1) + xs * _expand_heads(dskip_ref[...])
    zg = z_ref[0]
    y = y * (zg * jax.nn.sigmoid(zg))
    y = y * lax.rsqrt(jnp.mean(y * y, axis=-1, keepdims=True) + NORM_EPS) * nw_ref[...]
    o_ref[0] = y.astype(o_ref.dtype)


def ssd_mixer(z, xbc, dt, conv_w, conv_b, dt_bias, a_log, d_skip, norm_w):
    bsz, seq, di = z.shape
    cch = xbc.shape[-1]
    nc = seq // SSD_CHUNK
    pad = lambda v: jnp.pad(v.astype(F32), (0, LANES - v.shape[0])).reshape(1, LANES)
    full = lambda shape: pl.BlockSpec(shape, lambda b, c: (0,) * len(shape))
    return pl.pallas_call(
        _ssd_kernel,
        out_shape=jax.ShapeDtypeStruct((bsz, seq, di), BF16),
        grid=(bsz, nc),
        in_specs=[
            pl.BlockSpec((1, SSD_CHUNK, di), lambda b, c: (b, c, 0)),
            pl.BlockSpec((1, SSD_CHUNK, cch), lambda b, c: (b, c, 0)),
            pl.BlockSpec((1, SSD_CHUNK, LANES), lambda b, c: (b, c, 0)),
            full((SSD_CONV, cch)), full((1, cch)),
            full((1, LANES)), full((1, LANES)), full((1, LANES)), full((1, di)),
        ],
        out_specs=pl.BlockSpec((1, SSD_CHUNK, di), lambda b, c: (b, c, 0)),
        scratch_shapes=[pltpu.VMEM((SSD_STATE, di), F32), pltpu.VMEM((8, cch), F32)],
        compiler_params=_cparams(("parallel", "arbitrary")),
        name="ssd",
    )(z, xbc, dt, conv_w, conv_b.reshape(1, cch), pad(dt_bias), pad(a_log), pad(d_skip),
      norm_w.reshape(1, di))


def _sb_kernel(q_ref, k_ref, v_ref, o_ref, acc_ref, carry_ref):
    tq, tk = SB_Q_BLOCK, ATT_BLOCK
    qi = pl.program_id(2)
    sr = lax.broadcasted_iota(I32, (tk, tk), 0)
    sc = lax.broadcasted_iota(I32, (tk, tk), 1)
    strict = sc < sr
    suffix = jnp.where(sr > sc, 1.0, 0.0).astype(BF16)

    acc_ref[...] = jnp.zeros_like(acc_ref)
    carry_ref[...] = jnp.zeros_like(carry_ref)

    def block(j, r0, nr, masked):
        start = pl.multiple_of(j * tk, tk)
        rows = slice(r0, r0 + nr)
        for hh in range(2):
            lanes = slice(hh * HEAD_DIM, (hh + 1) * HEAD_DIM)
            kj = k_ref[0, pl.ds(start, tk), lanes]
            vj = v_ref[0, pl.ds(start, tk), lanes]
            z = _dot_nt(q_ref[0, rows, lanes], kj) * LOG2E
            sp = jnp.maximum(z, 0.0) + jnp.log2(1.0 + jnp.exp2(-jnp.abs(z)))
            spm = jnp.where(strict, sp, 0.0) if masked else sp
            carry = carry_ref[hh, rows, :]
            after = _dot(spm.astype(BF16), suffix) + jnp.concatenate([carry, carry], axis=1)
            att = jnp.exp2(z - sp - after)
            if masked:
                att = jnp.where(strict, att, 0.0)
            acc_ref[hh, rows, :] += _dot(att.astype(BF16), vj)
            carry_ref[hh, rows, :] = jnp.broadcast_to(after[:, 0:1] + spm[:, 0:1], (nr, LANES))

    nkb = (qi + 1) * (tq // tk)
    block(nkb - 1, tk, tk, True)
    block(nkb - 2, tk, tk, False)
    block(nkb - 2, 0, tk, True)

    def body(i, c):
        for d in range(tq // tk):
            block((qi - i) * (tq // tk) - 1 - d, 0, tq, False)
        return c

    lax.fori_loop(0, qi, body, 0)
    o_ref[0] = jnp.concatenate([acc_ref[0], acc_ref[1]], axis=1).astype(o_ref.dtype)


def sb_attention(qkv, bsz, seq):
    width = SB_HEADS * HEAD_DIM
    npair = width // LANES
    nq = seq // SB_Q_BLOCK
    assert SB_Q_BLOCK == 2 * ATT_BLOCK and seq % SB_Q_BLOCK == 0
    return pl.pallas_call(
        _sb_kernel,
        out_shape=jax.ShapeDtypeStruct((bsz, seq, width), BF16),
        grid=(bsz, npair, nq),
        in_specs=[
            pl.BlockSpec((1, SB_Q_BLOCK, LANES), lambda b, p, i: (b, i, p)),
            pl.BlockSpec((1, seq, LANES), lambda b, p, i: (b, 0, npair + p)),
            pl.BlockSpec((1, seq, LANES), lambda b, p, i: (b, 0, 2 * npair + p)),
        ],
        out_specs=pl.BlockSpec((1, SB_Q_BLOCK, LANES), lambda b, p, i: (b, i, p)),
        scratch_shapes=[pltpu.VMEM((2, SB_Q_BLOCK, HEAD_DIM), F32),
                        pltpu.VMEM((2, SB_Q_BLOCK, LANES), F32)],
        compiler_params=_cparams(("parallel", "parallel", "arbitrary")),
        name="stickbreak",
    )(qkv, qkv, qkv)


def _fold_rows(m):
    parts = [m[i * 8:(i + 1) * 8, :] for i in range(m.shape[0] // 8)]
    while len(parts) > 1:
        parts = [a + b for a, b in zip(parts[0::2], parts[1::2])]
    return parts[0]


def _ordered_to_float(u):
    key = u ^ INT_MIN
    bits = jnp.where(key < 0, key ^ 0x7FFFFFFF, key)
    return pltpu.bitcast(bits, F32)


def _dsa_kernel(dq_ref, dk_ref, dv_ref, iq_ref, ik_ref, iw_ref, o_ref,
                sc_ref, bias_ref, qs_ref, s_ref, mx_ref, l_ref, acc_ref):
    t = ATT_BLOCK
    qi = pl.program_id(1)
    nkb = qi + 1
    rpg = DSA_HEADS // DSA_KV_HEADS
    topk = float(DSA_MAX_TOPK)
    si = lax.broadcasted_iota(I32, (t, t), 0)
    ti = lax.broadcasted_iota(I32, (t, t), 1)

    for h in range(IDX_HEADS):
        qs_ref[h * t:(h + 1) * t, :] = iq_ref[0, :, h * IDX_DIM:(h + 1) * IDX_DIM]
    wt = (iw_ref[0] * (IDX_HEADS ** -0.5)).T

    def score_block(j, c):
        start = pl.multiple_of(j * t, t)
        ikj = ik_ref[0, pl.ds(start, t), 0:IDX_DIM]
        logits = _dot_nt(ikj, qs_ref[...])
        score = jnp.zeros((t, t), F32)
        for h in range(IDX_HEADS):
            score = score + wt[h:h + 1, :] * jnp.maximum(logits[:, h * t:(h + 1) * t], 0.0)
        causal = (j * t + si) <= (qi * t + ti)
        sc_ref[j] = jnp.where(causal, score, -jnp.inf)
        return c

    lax.fori_loop(0, nkb, score_block, 0)

    def count_ge(cand):
        def body(j, cnt):
            return cnt + _fold_rows(jnp.where(sc_ref[j] >= cand, 1.0, 0.0))
        cnt = lax.fori_loop(0, nkb, body, jnp.zeros((8, t), F32))
        return jnp.sum(cnt, axis=0, keepdims=True)

    @pl.when(qi == 0)
    def _():
        bias_ref[0] = jnp.where(ti <= si, 0.0, NEG_BIG)

    @pl.when(qi > 0)
    def _():
        def bit_step(i, c):
            thr_u, cnt_thr = c
            cand_u = thr_u | lax.shift_left(jnp.int32(1), 31 - i)
            cnt = count_ge(_ordered_to_float(cand_u))
            ok = cnt >= topk
            return jnp.where(ok, cand_u, thr_u), jnp.where(ok, cnt, cnt_thr)

        thr_u, cnt_thr = lax.fori_loop(
            0, 32, bit_step, (jnp.zeros((1, t), I32), jnp.full((1, t), topk, F32)))
        thr = _ordered_to_float(thr_u)
        excess = jnp.max(cnt_thr) > topk

        @pl.when(jnp.logical_not(excess))
        def _():
            def fill(j, c):
                bias_ref[j] = jnp.where(sc_ref[j] >= thr, 0.0, NEG_BIG).T
                return c
            lax.fori_loop(0, nkb, fill, 0)

        @pl.when(excess)
        def _():
            nxt = _ordered_to_float(thr_u + 1)
            need = topk - count_ge(nxt)
            incl = jnp.where(ti <= si, 1.0, 0.0).astype(BF16)

            def fix(j, seen):
                sc = sc_ref[j]
                gt = sc >= nxt
                eq = jnp.logical_and(sc >= thr, jnp.logical_not(gt))
                rank = _dot(incl, jnp.where(eq, 1.0, 0.0).astype(BF16)) + seen
                sel = jnp.logical_or(gt, jnp.logical_and(eq, rank <= need))
                bias_ref[j] = jnp.where(sel, 0.0, NEG_BIG).T
                return rank[t - 1:t, :]

            lax.fori_loop(0, nkb, fix, jnp.zeros((1, t), F32))

    for g in range(DSA_KV_HEADS):
        for r in range(rpg):
            h = g * rpg + r
            qs_ref[r * t:(r + 1) * t, :] = dq_ref[0, :, h * HEAD_DIM:(h + 1) * HEAD_DIM]
        lanes = slice(g * HEAD_DIM, (g + 1) * HEAD_DIM)
        mx_ref[...] = jnp.full(mx_ref.shape, NEG_BIG, F32)

        def pass_scores(j, c):
            start = pl.multiple_of(j * t, t)
            kj = dk_ref[0, pl.ds(start, t), lanes]
            s = _dot_nt(qs_ref[0:rpg * t, :], kj) + jnp.concatenate([bias_ref[j]] * rpg, axis=0)
            s_ref[j] = s
            mx_ref[...] = jnp.maximum(mx_ref[...], jnp.maximum(s[:, :LANES], s[:, LANES:]))
            return c

        lax.fori_loop(0, nkb, pass_scores, 0)
        m = jnp.max(mx_ref[...], axis=1, keepdims=True)
        mx_ref[...] = jnp.broadcast_to(m, mx_ref.shape)
        l_ref[...] = jnp.zeros_like(l_ref)
        acc_ref[...] = jnp.zeros_like(acc_ref)

        def pass_values(j, c):
            start = pl.multiple_of(j * t, t)
            vj = dv_ref[0, pl.ds(start, t), lanes]
            mb = mx_ref[...]
            p = jnp.exp(s_ref[j] - jnp.concatenate([mb, mb], axis=1))
            l_ref[...] += p[:, :LANES] + p[:, LANES:]
            acc_ref[...] += _dot(p.astype(BF16), vj)
            return c

        lax.fori_loop(0, nkb, pass_values, 0)
        out = acc_ref[...] / jnp.sum(l_ref[...], axis=1, keepdims=True)
        for r in range(rpg):
            h = g * rpg + r
            o_ref[0, :, h * HEAD_DIM:(h + 1) * HEAD_DIM] = out[r * t:(r + 1) * t, :].astype(o_ref.dtype)


def dsa_attention(dq, dk, dv, iq, ik, iw, bsz, seq):
    t = ATT_BLOCK
    nq = seq // t
    rpg = DSA_HEADS // DSA_KV_HEADS
    qblk = lambda w: pl.BlockSpec((1, t, w), lambda b, i: (b, i, 0))
    kblk = pl.BlockSpec((1, seq, LANES), lambda b, i: (b, 0, 0))
    return pl.pallas_call(
        _dsa_kernel,
        out_shape=jax.ShapeDtypeStruct((bsz, seq, DSA_HEADS * HEAD_DIM), BF16),
        grid=(bsz, nq),
        in_specs=[qblk(DSA_HEADS * HEAD_DIM), kblk, kblk, qblk(IDX_HEADS * IDX_DIM), kblk, qblk(LANES)],
        out_specs=qblk(DSA_HEADS * HEAD_DIM),
        scratch_shapes=[
            pltpu.VMEM((nq, t, t), F32),
            pltpu.VMEM((nq, t, t), F32),
            pltpu.VMEM((IDX_HEADS * t, HEAD_DIM), BF16),
            pltpu.VMEM((nq, rpg * t, t), F32),
            pltpu.VMEM((rpg * t, LANES), F32),
            pltpu.VMEM((rpg * t, LANES), F32),
            pltpu.VMEM((rpg * t, HEAD_DIM), F32),
        ],
        compiler_params=_cparams(("parallel", "arbitrary")),
        name="dsa",
    )(dq, dk, dv, iq, ik, iw)


def _merge_kernel(x_ref, mod_ref, ys_ref, yb_ref, yd_ref, g0_ref, g1_ref, g2_ref,
                  ws_ref, wb_ref, wd_ref, wo_ref, o_ref):
    merged = (jax.nn.sigmoid(g0_ref[...]) * _dot(ys_ref[...], ws_ref[...])
              + jax.nn.sigmoid(g1_ref[...]) * _dot(yb_ref[...], wb_ref[...])
              + jax.nn.sigmoid(g2_ref[...]) * _dot(yd_ref[...], wd_ref[...]))
    o_ref[...] = x_ref[...] + mod_ref[0, 2:3, :] * _dot(merged.astype(BF16), wo_ref[...])


def merge(x2d, mod6, y_ssd, y_sb, y_dsa, gates, w_ssd, w_sb, w_dsa, w_out, seq_len, tm=512):
    m, d = x2d.shape
    tiles_per_seq = seq_len // tm
    row = lambda w: pl.BlockSpec((tm, w), lambda i: (i, 0))
    full = lambda a: pl.BlockSpec(a.shape, lambda i: (0, 0))
    return pl.pallas_call(
        _merge_kernel,
        out_shape=jax.ShapeDtypeStruct((m, d), F32),
        grid=(m // tm,),
        in_specs=[
            row(d),
            pl.BlockSpec((1, 6, d), lambda i: (i // tiles_per_seq, 0, 0)),
            row(y_ssd.shape[1]), row(y_sb.shape[1]), row(y_dsa.shape[1]),
            pl.BlockSpec((tm, d), lambda i: (i, 0)),
            pl.BlockSpec((tm, d), lambda i: (i, 1)),
            pl.BlockSpec((tm, d), lambda i: (i, 2)),
            full(w_ssd), full(w_sb), full(w_dsa), full(w_out),
        ],
        out_specs=row(d),
        compiler_params=_cparams(("parallel",)),
        name="merge",
    )(x2d, mod6, y_ssd, y_sb, y_dsa, gates, gates, gates, w_ssd, w_sb, w_dsa, w_out)


def _mlp_kernel(x_ref, mod_ref, nw_ref, wu_ref, wd_ref, fw_ref, o_ref, h_ref, acc_ref, *, final):
    j = pl.program_id(1)

    @pl.when(j == 0)
    def _():
        h_ref[...] = _norm_mod(x_ref[...], nw_ref[...], mod_ref[0, 4:5, :],
                               mod_ref[0, 3:4, :]).astype(BF16)
        acc_ref[...] = jnp.zeros_like(acc_ref)

    u = jnp.maximum(_dot(h_ref[...], wu_ref[...]), 0.0)
    acc_ref[...] += _dot((u * u).astype(BF16), wd_ref[...])

    @pl.when(j == pl.num_programs(1) - 1)
    def _():
        y = x_ref[...] + mod_ref[0, 5:6, :] * acc_ref[...]
        if final:
            y = y * lax.rsqrt(jnp.mean(y * y, axis=-1, keepdims=True) + NORM_EPS) * fw_ref[...]
        o_ref[...] = y


def mlp(x2d, mod6, norm_w, w_up, w_down, final_w, seq_len, final, tm=1024, th=512):
    m, d = x2d.shape
    hid = w_up.shape[1]
    tiles_per_seq = seq_len // tm
    return pl.pallas_call(
        functools.partial(_mlp_kernel, final=final),
        out_shape=jax.ShapeDtypeStruct((m, d), F32),
        grid=(m // tm, hid // th),
        in_specs=[
            pl.BlockSpec((tm, d), lambda i, j: (i, 0)),
            pl.BlockSpec((1, 6, d), lambda i, j: (i // tiles_per_seq, 0, 0)),
            pl.BlockSpec((1, d), lambda i, j: (0, 0)),
            pl.BlockSpec((d, th), lambda i, j: (0, j)),
            pl.BlockSpec((th, d), lambda i, j: (j, 0)),
            pl.BlockSpec((1, d), lambda i, j: (0, 0)),
        ],
        out_specs=pl.BlockSpec((tm, d), lambda i, j: (i, 0)),
        scratch_shapes=[pltpu.VMEM((tm, d), BF16), pltpu.VMEM((tm, d), F32)],
        compiler_params=_cparams(("parallel", "arbitrary")),
        name="mlp",
    )(x2d, mod6, norm_w.reshape(1, d), w_up, w_down, final_w.reshape(1, d))


def _split_w_in(w):
    d = w.shape[0]
    di = SSD_HEADS * SSD_HEAD_DIM
    cch = di + 2 * SSD_GROUPS * SSD_STATE
    sbw = SB_HEADS * HEAD_DIM
    dsw = DSA_HEADS * HEAD_DIM
    kvw = DSA_KV_HEADS * HEAD_DIM
    sizes = [di, cch, SSD_HEADS, sbw, sbw, sbw, dsw, kvw, kvw, IDX_HEADS * IDX_DIM, IDX_DIM, IDX_HEADS]
    offs = np.cumsum([0] + sizes)
    seg = [w[:, offs[i]:offs[i + 1]] for i in range(len(sizes))]
    gate = w[:, offs[-1]:]
    z_w, xbc_w, dt_w, sq, sk, sv, dq, dk, dv, iq, ik, iw = seg
    zpad = lambda a, n: jnp.pad(a, ((0, 0), (0, n - a.shape[1])))
    scale = HEAD_DIM ** -0.5
    w_ssd = jnp.concatenate([z_w, xbc_w, zpad(dt_w, LANES)], axis=1)
    w_sb = jnp.concatenate([sq * scale, sk, sv], axis=1)
    w_dsa = jnp.concatenate([dq * scale, dk, dv, iq * (IDX_DIM ** -0.5), zpad(ik, LANES),
                             zpad(iw, LANES)], axis=1)
    return (w_ssd.astype(BF16), w_sb.astype(BF16), w_dsa.astype(BF16), gate.astype(BF16))


def kernel(x, c, norm1_w, ada_w, ada_b, w_in, conv_w, conv_b, dt_bias, a_log, d_skip, ssd_norm_w,
           w_br_ssd, w_br_sb, w_br_dsa, w_out, norm2_w, w_up, w_down, final_norm_w):
    bsz, seq, d = x.shape
    depth = ada_w.shape[0]
    m = bsz * seq
    di = SSD_HEADS * SSD_HEAD_DIM
    cch = di + 2 * SSD_GROUPS * SSD_STATE
    dsw = DSA_HEADS * HEAD_DIM
    rope = rope_tables(seq)
    mods = ada_mod(c, ada_w, ada_b).reshape(depth, bsz, 6, d)
    x2d = x.reshape(m, d)
    for l in range(depth):
        mod6 = mods[l]
        w_ssd, w_sb, w_dsa, w_gate = _split_w_in(w_in[l])
        z, xbc, dt = proj(x2d, mod6, norm1_w[l], w_ssd, (di, cch, LANES), (F32, F32, F32), seq)
        (qkv,) = proj(x2d, mod6, norm1_w[l], w_sb, (3 * SB_HEADS * HEAD_DIM,), (BF16,), seq)
        dq, dk, dv, iq, ik, iw = proj(
            x2d, mod6, norm1_w[l], w_dsa, (dsw, LANES, LANES, dsw, LANES, LANES),
            (BF16, BF16, BF16, BF16, BF16, F32), seq, rope=rope, rope_chunks=(0, 1, 2, 3, 4, 6, 7, 8, 9, 10))
        (gates,) = proj(x2d, mod6, norm1_w[l], w_gate, (3 * d,), (F32,), seq)

        r3 = lambda a: a.reshape(bsz, seq, a.shape[-1])
        y_ssd = ssd_mixer(r3(z), r3(xbc), r3(dt), conv_w[l], conv_b[l], dt_bias[l], a_log[l],
                          d_skip[l], ssd_norm_w[l])
        y_sb = sb_attention(r3(qkv), bsz, seq)
        y_dsa = dsa_attention(r3(dq), r3(dk), r3(dv), r3(iq), r3(ik), r3(iw), bsz, seq)

        x2d = merge(x2d, mod6, y_ssd.reshape(m, di), y_sb.reshape(m, -1), y_dsa.reshape(m, -1), gates,
                    w_br_ssd[l].astype(BF16), w_br_sb[l].astype(BF16), w_br_dsa[l].astype(BF16),
                    w_out[l].astype(BF16), seq)
        x2d = mlp(x2d, mod6, norm2_w[l], w_up[l].astype(BF16), w_down[l].astype(BF16), final_norm_w,
                  seq, final=(l == depth - 1))
    return x2d.reshape(bsz, seq, d)
```

```python
import functools
import math

import jax
import jax.numpy as jnp
import numpy as np
from jax import lax
from jax.experimental import pallas as pl
from jax.experimental.pallas import tpu as pltpu

F32 = jnp.float32
BF16 = jnp.bfloat16
I32 = jnp.int32

NORM_EPS = 1e-6
HEAD_DIM = 64
ROPE_THETA = 500000.0
LANES = 128

SSD_HEADS = 16
SSD_GROUPS = 2
SSD_STATE = 128
SSD_CHUNK = 128
SSD_HEAD_DIM = 64
SSD_CONV = 4

SB_HEADS = 8
DSA_HEADS = 8
DSA_KV_HEADS = 2
IDX_HEADS = 8
IDX_DIM = 64
DSA_MAX_TOPK = 256

ATT_BLOCK = 256
SB_Q_BLOCK = 512
VMEM_LIMIT = 56 * 1024 * 1024

LOG2E = 1.4426950408889634
INT_MIN = -(2 ** 31)
NEG_BIG = -1e30


def _cparams(sem):
    return pltpu.CompilerParams(dimension_semantics=sem, vmem_limit_bytes=VMEM_LIMIT)


def _dot(a, b):
    return jnp.dot(a, b, preferred_element_type=F32)


def _dot_nt(a, b):
    return lax.dot_general(a, b, (((1,), (1,)), ((), ())), preferred_element_type=F32)


def _softplus(x):
    return jnp.maximum(x, 0.0) + jnp.log1p(jnp.exp(-jnp.abs(x)))


def _norm_mod(x, w, scale, shift):
    y = x * lax.rsqrt(jnp.mean(x * x, axis=-1, keepdims=True) + NORM_EPS)
    return (y * w) * (1.0 + scale) + shift


def _ada_kernel(c_ref, w_ref, b_ref, o_ref):
    c = c_ref[...]
    a = (c * jax.nn.sigmoid(c)).astype(BF16)
    o_ref[0] = _dot(a, w_ref[0].astype(BF16)) + b_ref[0]


def ada_mod(c, ada_w, ada_b, tn=1536):
    depth, d, n = ada_w.shape
    bsz = c.shape[0]
    return pl.pallas_call(
        _ada_kernel,
        out_shape=jax.ShapeDtypeStruct((depth, bsz, n), F32),
        grid=(depth, n // tn),
        in_specs=[
            pl.BlockSpec((bsz, d), lambda l, j: (0, 0)),
            pl.BlockSpec((1, d, tn), lambda l, j: (l, 0, j)),
            pl.BlockSpec((1, 1, tn), lambda l, j: (l, 0, j)),
        ],
        out_specs=pl.BlockSpec((1, bsz, tn), lambda l, j: (l, 0, j)),
        compiler_params=_cparams(("parallel", "parallel")),
        name="ada_mod",
    )(c, ada_w, ada_b.reshape(depth, 1, n))


def _proj_kernel(*refs, splits, rope_chunks, has_rope):
    if has_rope:
        x_ref, mod_ref, nw_ref, w_ref, cos_ref, s1_ref, s2_ref = refs[:7]
        out_refs = refs[7:]
    else:
        x_ref, mod_ref, nw_ref, w_ref = refs[:4]
        out_refs = refs[4:]
    h = _norm_mod(x_ref[...], nw_ref[...], mod_ref[0, 1:2, :], mod_ref[0, 0:1, :]).astype(BF16)
    off = 0
    for o_ref, width in zip(out_refs, splits):
        acc = _dot(h, w_ref[:, off:off + width])
        if has_rope:
            cos, s1, s2 = cos_ref[...], s1_ref[...], s2_ref[...]
            pieces = []
            for c in range(width // LANES):
                xc = acc[:, c * LANES:(c + 1) * LANES]
                if (off // LANES + c) in rope_chunks:
                    xc = (xc * cos + pltpu.roll(xc, LANES - 8, axis=1) * s1
                          + pltpu.roll(xc, 8, axis=1) * s2)
                pieces.append(xc)
            acc = pieces[0] if len(pieces) == 1 else jnp.concatenate(pieces, axis=1)
        o_ref[...] = acc.astype(o_ref.dtype)
        off += width


def proj(x2d, mod6, norm_w, w, splits, dtypes, seq_len, rope=None, rope_chunks=(), tm=512):
    m, d = x2d.shape
    n = w.shape[1]
    assert sum(splits) == n and m % tm == 0 and seq_len % tm == 0
    tiles_per_seq = seq_len // tm
    in_specs = [
        pl.BlockSpec((tm, d), lambda i: (i, 0)),
        pl.BlockSpec((1, 6, d), lambda i: (i // tiles_per_seq, 0, 0)),
        pl.BlockSpec((1, d), lambda i: (0, 0)),
        pl.BlockSpec((d, n), lambda i: (0, 0)),
    ]
    args = [x2d, mod6, norm_w.reshape(1, d), w]
    if rope is not None:
        for t in rope:
            in_specs.append(pl.BlockSpec((tm, LANES), lambda i: (i % tiles_per_seq, 0)))
            args.append(t)
    kern = functools.partial(_proj_kernel, splits=tuple(splits), rope_chunks=tuple(rope_chunks),
                             has_rope=rope is not None)
    return pl.pallas_call(
        kern,
        out_shape=[jax.ShapeDtypeStruct((m, s), dt) for s, dt in zip(splits, dtypes)],
        grid=(m // tm,),
        in_specs=in_specs,
        out_specs=[pl.BlockSpec((tm, s), lambda i: (i, 0)) for s in splits],
        compiler_params=_cparams(("parallel",)),
        name="proj",
    )(*args)


def rope_tables(seq_len):
    rot = HEAD_DIM // 4
    half = rot // 2
    inv_freq = jnp.exp(jnp.arange(half, dtype=F32) * (-2.0 * math.log(ROPE_THETA) / rot))
    ang = jnp.arange(seq_len, dtype=jnp.int32).astype(F32)[:, None] * inv_freq[None, :]
    cos, sin = jnp.cos(ang), jnp.sin(ang)
    ones = jnp.ones((seq_len, HEAD_DIM - rot), F32)
    zeros = jnp.zeros((seq_len, HEAD_DIM - rot), F32)
    zh = jnp.zeros((seq_len, half), F32)
    c64 = jnp.concatenate([cos, cos, ones], axis=1)
    s1_64 = jnp.concatenate([-sin, zh, zeros], axis=1)
    s2_64 = jnp.concatenate([zh, sin, zeros], axis=1)
    tile = lambda a: jnp.concatenate([a, a], axis=1)
    return tile(c64), tile(s1_64), tile(s2_64)


def _expand_heads(a):
    q = a.shape[0]
    lane = lax.broadcasted_iota(I32, (q, LANES), 1)
    chunks = []
    for c in range(SSD_HEADS // 2):
        lo = jnp.broadcast_to(a[:, 2 * c:2 * c + 1], (q, LANES))
        hi = jnp.broadcast_to(a[:, 2 * c + 1:2 * c + 2], (q, LANES))
        chunks.append(jnp.where(lane < SSD_HEAD_DIM, lo, hi))
    return jnp.concatenate(chunks, axis=1)


def _ssd_kernel(z_ref, xbc_ref, dt_ref, cw_ref, cb_ref, dtb_ref, alog_ref, dskip_ref, nw_ref,
                o_ref, state_ref, tail_ref):
    q = SSD_CHUNK
    di = SSD_HEADS * SSD_HEAD_DIM
    gw = SSD_STATE
    hpg = SSD_HEADS // SSD_GROUPS

    @pl.when(pl.program_id(1) == 0)
    def _():
        state_ref[...] = jnp.zeros_like(state_ref)
        tail_ref[...] = jnp.zeros_like(tail_ref)

    cur = xbc_ref[0]
    row = lax.broadcasted_iota(I32, (q, 1), 0)
    tail = tail_ref[...]
    conv = cb_ref[...] + cw_ref[SSD_CONV - 1:SSD_CONV, :] * cur
    for j in range(1, SSD_CONV):
        rolled = pltpu.roll(cur, j, axis=0)
        patch = jnp.tile(pltpu.roll(tail, j, axis=0), (q // 8, 1))
        shifted = jnp.where(row < j, patch, rolled)
        conv = conv + cw_ref[SSD_CONV - 1 - j:SSD_CONV - j, :] * shifted
    tail_ref[...] = cur[q - 8:, :]
    xbc = conv * jax.nn.sigmoid(conv)
    xs = xbc[:, :di]
    bm = xbc[:, di:di + SSD_GROUPS * gw]
    cm = xbc[:, di + SSD_GROUPS * gw:]

    dt = _softplus(dt_ref[0] + dtb_ref[...])
    da = dt * (-jnp.exp(alog_ref[...]))
    acum = da
    s = 1
    while s < q:
        acum = acum + jnp.where(row >= s, pltpu.roll(acum, s, axis=0), 0.0)
        s *= 2
    acum_t = acum.T
    a_last = acum[q - 1:q, :]
    e_in = jnp.exp(acum)
    e_out = jnp.exp(a_last - acum)

    dt_x = _expand_heads(dt)
    e_in_x = _expand_heads(e_in)
    e_out_x = _expand_heads(e_out)
    e_last_x = _expand_heads(jnp.exp(a_last))

    xdt = xs * dt_x
    xdt_b = xdt.astype(BF16)
    xdt_out_b = (xdt * e_out_x).astype(BF16)

    ri = lax.broadcasted_iota(I32, (q, q), 0)
    ci = lax.broadcasted_iota(I32, (q, q), 1)
    causal = ri >= ci

    y_parts = []
    new_states = []
    for g in range(SSD_GROUPS):
        bm_g = bm[:, g * gw:(g + 1) * gw]
        cm_g = cm[:, g * gw:(g + 1) * gw].astype(BF16)
        bm_gb = bm_g.astype(BF16)
        cb = _dot_nt(cm_g, bm_gb)
        sl = slice(g * hpg * SSD_HEAD_DIM, (g + 1) * hpg * SSD_HEAD_DIM)
        st_g = state_ref[:, sl]
        y_off = _dot(cm_g, st_g.astype(BF16)) * e_in_x[:, sl]
        for r in range(hpg):
            h = g * hpg + r
            diff = acum[:, h:h + 1] - acum_t[h:h + 1, :]
            lmat = jnp.exp(jnp.where(causal, diff, NEG_BIG))
            m_h = (cb * lmat).astype(BF16)
            y_parts.append(_dot(m_h, xdt_b[:, h * SSD_HEAD_DIM:(h + 1) * SSD_HEAD_DIM])
                           + y_off[:, r * SSD_HEAD_DIM:(r + 1) * SSD_HEAD_DIM])
        upd = _dot(bm_g.T.astype(BF16), xdt_out_b[:, sl])
        new_states.append(st_g * e_last_x[:, sl] + upd)
    for g in range(SSD_GROUPS):
        sl = slice(g * hpg * SSD_HEAD_DIM, (g + 1) * hpg * SSD_HEAD_DIM)
        state_ref[:, sl] = new_states[g]

    y = jnp.concatenate(y_parts, axis=1) + xs * _expand_heads(dskip_ref[...])
    zg = z_ref[0]
    y = y * (zg * jax.nn.sigmoid(zg))
    y = y * lax.rsqrt(jnp.mean(y * y, axis=-1, keepdims=True) + NORM_EPS) * nw_ref[...]
    o_ref[0] = y.astype(o_ref.dtype)


def ssd_mixer(z, xbc, dt, conv_w, conv_b, dt_bias, a_log, d_skip, norm_w):
    bsz, seq, di = z.shape
    cch = xbc.shape[-1]
    nc = seq // SSD_CHUNK
    pad = lambda v: jnp.pad(v.astype(F32), (0, LANES - v.shape[0])).reshape(1, LANES)
    full = lambda shape: pl.BlockSpec(shape, lambda b, c: (0,) * len(shape))
    return pl.pallas_call(
        _ssd_kernel,
        out_shape=jax.ShapeDtypeStruct((bsz, seq, di), BF16),
        grid=(bsz, nc),
        in_specs=[
            pl.BlockSpec((1, SSD_CHUNK, di), lambda b, c: (b, c, 0)),
            pl.BlockSpec((1, SSD_CHUNK, cch), lambda b, c: (b, c, 0)),
            pl.BlockSpec((1, SSD_CHUNK, LANES), lambda b, c: (b, c, 0)),
            full((SSD_CONV, cch)), full((1, cch)),
            full((1, LANES)), full((1, LANES)), full((1, LANES)), full((1, di)),
        ],
        out_specs=pl.BlockSpec((1, SSD_CHUNK, di), lambda b, c: (b, c, 0)),
        scratch_shapes=[pltpu.VMEM((SSD_STATE, di), F32), pltpu.VMEM((8, cch), F32)],
        compiler_params=_cparams(("parallel", "arbitrary")),
        name="ssd",
    )(z, xbc, dt, conv_w, conv_b.reshape(1, cch), pad(dt_bias), pad(a_log), pad(d_skip),
      norm_w.reshape(1, di))


def _sb_kernel(q_ref, k_ref, v_ref, o_ref, acc_ref, carry_ref):
    tq, tk = SB_Q_BLOCK, ATT_BLOCK
    qi = pl.program_id(2)
    sr = lax.broadcasted_iota(I32, (tk, tk), 0)
    sc = lax.broadcasted_iota(I32, (tk, tk), 1)
    strict = sc < sr
    suffix = jnp.where(sr > sc, 1.0, 0.0).astype(BF16)

    acc_ref[...] = jnp.zeros_like(acc_ref)
    carry_ref[...] = jnp.zeros_like(carry_ref)

    def block(j, r0, nr, masked):
        start = pl.multiple_of(j * tk, tk)
        rows = slice(r0, r0 + nr)
        for hh in range(2):
            lanes = slice(hh * HEAD_DIM, (hh + 1) * HEAD_DIM)
            kj = k_ref[0, pl.ds(start, tk), lanes]
            vj = v_ref[0, pl.ds(start, tk), lanes]
            z = _dot_nt(q_ref[0, rows, lanes], kj) * LOG2E
            sp = jnp.maximum(z, 0.0) + jnp.log2(1.0 + jnp.exp2(-jnp.abs(z)))
            spm = jnp.where(strict, sp, 0.0) if masked else sp
            carry = carry_ref[hh, rows, :]
            after = _dot(spm.astype(BF16), suffix) + jnp.concatenate([carry, carry], axis=1)
            att = jnp.exp2(z - sp - after)
            if masked:
                att = jnp.where(strict, att, 0.0)
            acc_ref[hh, rows, :] += _dot(att.astype(BF16), vj)
            carry_ref[hh, rows, :] = jnp.broadcast_to(after[:, 0:1] + spm[:, 0:1], (nr, LANES))

    nkb = (qi + 1) * (tq // tk)
    block(nkb - 1, tk, tk, True)
    block(nkb - 2, tk, tk, False)
    block(nkb - 2, 0, tk, True)

    def body(i, c):
        for d in range(tq // tk):
            block((qi - i) * (tq // tk) - 1 - d, 0, tq, False)
        return c

    lax.fori_loop(0, qi, body, 0)
    o_ref[0] = jnp.concatenate([acc_ref[0], acc_ref[1]], axis=1).astype(o_ref.dtype)


def sb_attention(qkv, bsz, seq):
    width = SB_HEADS * HEAD_DIM
    npair = width // LANES
    nq = seq // SB_Q_BLOCK
    assert SB_Q_BLOCK == 2 * ATT_BLOCK and seq % SB_Q_BLOCK == 0
    return pl.pallas_call(
        _sb_kernel,
        out_shape=jax.ShapeDtypeStruct((bsz, seq, width), BF16),
        grid=(bsz, npair, nq),
        in_specs=[
            pl.BlockSpec((1, SB_Q_BLOCK, LANES), lambda b, p, i: (b, i, p)),
            pl.BlockSpec((1, seq, LANES), lambda b, p, i: (b, 0, npair + p)),
            pl.BlockSpec((1, seq, LANES), lambda b, p, i: (b, 0, 2 * npair + p)),
        ],
        out_specs=pl.BlockSpec((1, SB_Q_BLOCK, LANES), lambda b, p, i: (b, i, p)),
        scratch_shapes=[pltpu.VMEM((2, SB_Q_BLOCK, HEAD_DIM), F32),
                        pltpu.VMEM((2, SB_Q_BLOCK, LANES), F32)],
        compiler_params=_cparams(("parallel", "parallel", "arbitrary")),
        name="stickbreak",
    )(qkv, qkv, qkv)


def _fold_rows(m):
    parts = [m[i * 8:(i + 1) * 8, :] for i in range(m.shape[0] // 8)]
    while len(parts) > 1:
        parts = [a + b for a, b in zip(parts[0::2], parts[1::2])]
    return parts[0]


def _ordered_to_float(u):
    key = u ^ INT_MIN
    bits = jnp.where(key < 0, key ^ 0x7FFFFFFF, key)
    return pltpu.bitcast(bits, F32)


def _dsa_kernel(dq_ref, dk_ref, dv_ref, iq_ref, ik_ref, iw_ref, o_ref,
                sc_ref, bias_ref, qs_ref, s_ref, mx_ref, l_ref, acc_ref):
    t = ATT_BLOCK
    qi = pl.program_id(1)
    nkb = qi + 1
    rpg = DSA_HEADS // DSA_KV_HEADS
    topk = float(DSA_MAX_TOPK)
    si = lax.broadcasted_iota(I32, (t, t), 0)
    ti = lax.broadcasted_iota(I32, (t, t), 1)

    for h in range(IDX_HEADS):
        qs_ref[h * t:(h + 1) * t, :] = iq_ref[0, :, h * IDX_DIM:(h + 1) * IDX_DIM]
    wt = (iw_ref[0] * (IDX_HEADS ** -0.5)).T

    def score_block(j, c):
        start = pl.multiple_of(j * t, t)
        ikj = ik_ref[0, pl.ds(start, t), 0:IDX_DIM]
        logits = _dot_nt(ikj, qs_ref[...])
        score = jnp.zeros((t, t), F32)
        for h in range(IDX_HEADS):
            score = score + wt[h:h + 1, :] * jnp.maximum(logits[:, h * t:(h + 1) * t], 0.0)
        causal = (j * t + si) <= (qi * t + ti)
        sc_ref[j] = jnp.where(causal, score, -jnp.inf)
        return c

    lax.fori_loop(0, nkb, score_block, 0)

    def count_ge(cand):
        def body(j, cnt):
            return cnt + _fold_rows(jnp.where(sc_ref[j] >= cand, 1.0, 0.0))
        cnt = lax.fori_loop(0, nkb, body, jnp.zeros((8, t), F32))
        return jnp.sum(cnt, axis=0, keepdims=True)

    @pl.when(qi == 0)
    def _():
        bias_ref[0] = jnp.where(ti <= si, 0.0, NEG_BIG)

    @pl.when(qi > 0)
    def _():
        def bit_step(i, c):
            thr_u, cnt_thr = c
            cand_u = thr_u | lax.shift_left(jnp.int32(1), 31 - i)
            cnt = count_ge(_ordered_to_float(cand_u))
            ok = cnt >= topk
            return jnp.where(ok, cand_u, thr_u), jnp.where(ok, cnt, cnt_thr)

        thr_u, cnt_thr = lax.fori_loop(
            0, 32, bit_step, (jnp.zeros((1, t), I32), jnp.full((1, t), topk, F32)))
        thr = _ordered_to_float(thr_u)
        excess = jnp.max(cnt_thr) > topk

        @pl.when(jnp.logical_not(excess))
        def _():
            def fill(j, c):
                bias_ref[j] = jnp.where(sc_ref[j] >= thr, 0.0, NEG_BIG).T
                return c
            lax.fori_loop(0, nkb, fill, 0)

        @pl.when(excess)
        def _():
            nxt = _ordered_to_float(thr_u + 1)
            need = topk - count_ge(nxt)
            incl = jnp.where(ti <= si, 1.0, 0.0).astype(BF16)

            def fix(j, seen):
                sc = sc_ref[j]
                gt = sc >= nxt
                eq = jnp.logical_and(sc >= thr, jnp.logical_not(gt))
                rank = _dot(incl, jnp.where(eq, 1.0, 0.0).astype(BF16)) + seen
                sel = jnp.logical_or(gt, jnp.logical_and(eq, rank <= need))
                bias_ref[j] = jnp.where(sel, 0.0, NEG_BIG).T
                return rank[t - 1:t, :]

            lax.fori_loop(0, nkb, fix, jnp.zeros((1, t), F32))

    for h in range(DSA_HEADS):
        qs_ref[h * t:(h + 1) * t, :] = dq_ref[0, :, h * HEAD_DIM:(h + 1) * HEAD_DIM]
    mx_ref[...] = jnp.full(mx_ref.shape, NEG_BIG, F32)
    grp = [(slice(g * rpg * t, (g + 1) * rpg * t), slice(g * HEAD_DIM, (g + 1) * HEAD_DIM))
           for g in range(DSA_KV_HEADS)]

    def pass_scores(j, c):
        start = pl.multiple_of(j * t, t)
        bias = jnp.concatenate([bias_ref[j]] * rpg, axis=0)
        for rows, lanes in grp:
            kj = dk_ref[0, pl.ds(start, t), lanes]
            s = _dot_nt(qs_ref[rows, :], kj) + bias
            s_ref[j, rows, :] = s
            mx_ref[rows, :] = jnp.maximum(mx_ref[rows, :], jnp.maximum(s[:, :LANES], s[:, LANES:]))
        return c

    lax.fori_loop(0, nkb, pass_scores, 0)
    m = jnp.max(mx_ref[...], axis=1, keepdims=True)
    mx_ref[...] = jnp.broadcast_to(m, mx_ref.shape)
    l_ref[...] = jnp.zeros_like(l_ref)
    acc_ref[...] = jnp.zeros_like(acc_ref)

    def pass_values(j, c):
        start = pl.multiple_of(j * t, t)
        for rows, lanes in grp:
            vj = dv_ref[0, pl.ds(start, t), lanes]
            mb = mx_ref[rows, :]
            p = jnp.exp(s_ref[j, rows, :] - jnp.concatenate([mb, mb], axis=1))
            l_ref[rows, :] += p[:, :LANES] + p[:, LANES:]
            acc_ref[rows, :] += _dot(p.astype(BF16), vj)
        return c

    lax.fori_loop(0, nkb, pass_values, 0)
    out = acc_ref[...] / jnp.sum(l_ref[...], axis=1, keepdims=True)
    for h in range(DSA_HEADS):
        o_ref[0, :, h * HEAD_DIM:(h + 1) * HEAD_DIM] = out[h * t:(h + 1) * t, :].astype(o_ref.dtype)


def dsa_attention(dq, dk, dv, iq, ik, iw, bsz, seq):
    t = ATT_BLOCK
    nq = seq // t
    rpg = DSA_HEADS // DSA_KV_HEADS
    qblk = lambda w: pl.BlockSpec((1, t, w), lambda b, i: (b, i, 0))
    kblk = pl.BlockSpec((1, seq, LANES), lambda b, i: (b, 0, 0))
    return pl.pallas_call(
        _dsa_kernel,
        out_shape=jax.ShapeDtypeStruct((bsz, seq, DSA_HEADS * HEAD_DIM), BF16),
        grid=(bsz, nq),
        in_specs=[qblk(DSA_HEADS * HEAD_DIM), kblk, kblk, qblk(IDX_HEADS * IDX_DIM), kblk, qblk(LANES)],
        out_specs=qblk(DSA_HEADS * HEAD_DIM),
        scratch_shapes=[
            pltpu.VMEM((nq, t, t), F32),
            pltpu.VMEM((nq, t, t), F32),
            pltpu.VMEM((IDX_HEADS * t, HEAD_DIM), BF16),
            pltpu.VMEM((nq, DSA_HEADS * t, t), F32),
            pltpu.VMEM((DSA_HEADS * t, LANES), F32),
            pltpu.VMEM((DSA_HEADS * t, LANES), F32),
            pltpu.VMEM((DSA_HEADS * t, HEAD_DIM), F32),
        ],
        compiler_params=_cparams(("parallel", "arbitrary")),
        name="dsa",
    )(dq, dk, dv, iq, ik, iw)


def _merge_kernel(x_ref, mod_ref, ys_ref, yb_ref, yd_ref, g0_ref, g1_ref, g2_ref,
                  ws_ref, wb_ref, wd_ref, wo_ref, o_ref):
    merged = (jax.nn.sigmoid(g0_ref[...]) * _dot(ys_ref[...], ws_ref[...])
              + jax.nn.sigmoid(g1_ref[...]) * _dot(yb_ref[...], wb_ref[...])
              + jax.nn.sigmoid(g2_ref[...]) * _dot(yd_ref[...], wd_ref[...]))
    o_ref[...] = x_ref[...] + mod_ref[0, 2:3, :] * _dot(merged.astype(BF16), wo_ref[...])


def merge(x2d, mod6, y_ssd, y_sb, y_dsa, gates, w_ssd, w_sb, w_dsa, w_out, seq_len, tm=512):
    m, d = x2d.shape
    tiles_per_seq = seq_len // tm
    row = lambda w: pl.BlockSpec((tm, w), lambda i: (i, 0))
    full = lambda a: pl.BlockSpec(a.shape, lambda i: (0, 0))
    return pl.pallas_call(
        _merge_kernel,
        out_shape=jax.ShapeDtypeStruct((m, d), F32),
        grid=(m // tm,),
        in_specs=[
            row(d),
            pl.BlockSpec((1, 6, d), lambda i: (i // tiles_per_seq, 0, 0)),
            row(y_ssd.shape[1]), row(y_sb.shape[1]), row(y_dsa.shape[1]),
            pl.BlockSpec((tm, d), lambda i: (i, 0)),
            pl.BlockSpec((tm, d), lambda i: (i, 1)),
            pl.BlockSpec((tm, d), lambda i: (i, 2)),
            full(w_ssd), full(w_sb), full(w_dsa), full(w_out),
        ],
        out_specs=row(d),
        compiler_params=_cparams(("parallel",)),
        name="merge",
    )(x2d, mod6, y_ssd, y_sb, y_dsa, gates, gates, gates, w_ssd, w_sb, w_dsa, w_out)


def _mlp_kernel(x_ref, mod_ref, nw_ref, wu_ref, wd_ref, fw_ref, o_ref, h_ref, acc_ref, *, final):
    j = pl.program_id(1)

    @pl.when(j == 0)
    def _():
        h_ref[...] = _norm_mod(x_ref[...], nw_ref[...], mod_ref[0, 4:5, :],
                               mod_ref[0, 3:4, :]).astype(BF16)
        acc_ref[...] = jnp.zeros_like(acc_ref)

    u = jnp.maximum(_dot(h_ref[...], wu_ref[...]), 0.0)
    acc_ref[...] += _dot((u * u).astype(BF16), wd_ref[...])

    @pl.when(j == pl.num_programs(1) - 1)
    def _():
        y = x_ref[...] + mod_ref[0, 5:6, :] * acc_ref[...]
        if final:
            y = y * lax.rsqrt(jnp.mean(y * y, axis=-1, keepdims=True) + NORM_EPS) * fw_ref[...]
        o_ref[...] = y


def mlp(x2d, mod6, norm_w, w_up, w_down, final_w, seq_len, final, tm=1024, th=1024):
    m, d = x2d.shape
    hid = w_up.shape[1]
    tiles_per_seq = seq_len // tm
    return pl.pallas_call(
        functools.partial(_mlp_kernel, final=final),
        out_shape=jax.ShapeDtypeStruct((m, d), F32),
        grid=(m // tm, hid // th),
        in_specs=[
            pl.BlockSpec((tm, d), lambda i, j: (i, 0)),
            pl.BlockSpec((1, 6, d), lambda i, j: (i // tiles_per_seq, 0, 0)),
            pl.BlockSpec((1, d), lambda i, j: (0, 0)),
            pl.BlockSpec((d, th), lambda i, j: (0, j)),
            pl.BlockSpec((th, d), lambda i, j: (j, 0)),
            pl.BlockSpec((1, d), lambda i, j: (0, 0)),
        ],
        out_specs=pl.BlockSpec((tm, d), lambda i, j: (i, 0)),
        scratch_shapes=[pltpu.VMEM((tm, d), BF16), pltpu.VMEM((tm, d), F32)],
        compiler_params=_cparams(("parallel", "arbitrary")),
        name="mlp",
    )(x2d, mod6, norm_w.reshape(1, d), w_up, w_down, final_w.reshape(1, d))


def _split_w_in(w):
    d = w.shape[0]
    di = SSD_HEADS * SSD_HEAD_DIM
    cch = di + 2 * SSD_GROUPS * SSD_STATE
    sbw = SB_HEADS * HEAD_DIM
    dsw = DSA_HEADS * HEAD_DIM
    kvw = DSA_KV_HEADS * HEAD_DIM
    sizes = [di, cch, SSD_HEADS, sbw, sbw, sbw, dsw, kvw, kvw, IDX_HEADS * IDX_DIM, IDX_DIM, IDX_HEADS]
    offs = np.cumsum([0] + sizes)
    seg = [w[:, offs[i]:offs[i + 1]] for i in range(len(sizes))]
    gate = w[:, offs[-1]:]
    z_w, xbc_w, dt_w, sq, sk, sv, dq, dk, dv, iq, ik, iw = seg
    zpad = lambda a, n: jnp.pad(a, ((0, 0), (0, n - a.shape[1])))
    scale = HEAD_DIM ** -0.5
    w_ssd = jnp.concatenate([z_w, xbc_w, zpad(dt_w, LANES)], axis=1)
    w_sb = jnp.concatenate([sq * scale, sk, sv], axis=1)
    w_dsa = jnp.concatenate([dq * scale, dk, dv, iq * (IDX_DIM ** -0.5), zpad(ik, LANES),
                             zpad(iw, LANES)], axis=1)
    return (w_ssd.astype(BF16), w_sb.astype(BF16), w_dsa.astype(BF16), gate.astype(BF16))


def kernel(x, c, norm1_w, ada_w, ada_b, w_in, conv_w, conv_b, dt_bias, a_log, d_skip, ssd_norm_w,
           w_br_ssd, w_br_sb, w_br_dsa, w_out, norm2_w, w_up, w_down, final_norm_w):
    bsz, seq, d = x.shape
    depth = ada_w.shape[0]
    m = bsz * seq
    di = SSD_HEADS * SSD_HEAD_DIM
    cch = di + 2 * SSD_GROUPS * SSD_STATE
    dsw = DSA_HEADS * HEAD_DIM
    rope = rope_tables(seq)
    mods = ada_mod(c, ada_w, ada_b).reshape(depth, bsz, 6, d)
    x2d = x.reshape(m, d)
    for l in range(depth):
        mod6 = mods[l]
        w_ssd, w_sb, w_dsa, w_gate = _split_w_in(w_in[l])
        z, xbc, dt = proj(x2d, mod6, norm1_w[l], w_ssd, (di, cch, LANES), (F32, F32, F32), seq)
        (qkv,) = proj(x2d, mod6, norm1_w[l], w_sb, (3 * SB_HEADS * HEAD_DIM,), (BF16,), seq)
        dq, dk, dv, iq, ik, iw = proj(
            x2d, mod6, norm1_w[l], w_dsa, (dsw, LANES, LANES, dsw, LANES, LANES),
            (BF16, BF16, BF16, BF16, BF16, F32), seq, rope=rope, rope_chunks=(0, 1, 2, 3, 4, 6, 7, 8, 9, 10))
        (gates,) = proj(x2d, mod6, norm1_w[l], w_gate, (3 * d,), (F32,), seq)

        r3 = lambda a: a.reshape(bsz, seq, a.shape[-1])
        y_ssd = ssd_mixer(r3(z), r3(xbc), r3(dt), conv_w[l], conv_b[l], dt_bias[l], a_log[l],
                          d_skip[l], ssd_norm_w[l])
        y_sb = sb_attention(r3(qkv), bsz, seq)
        y_dsa = dsa_attention(r3(dq), r3(dk), r3(dv), r3(iq), r3(ik), r3(iw), bsz, seq)

        x2d = merge(x2d, mod6, y_ssd.reshape(m, di), y_sb.reshape(m, -1), y_dsa.reshape(m, -1), gates,
                    w_br_ssd[l].astype(BF16), w_br_sb[l].astype(BF16), w_br_dsa[l].astype(BF16),
                    w_out[l].astype(BF16), seq)
        x2d = mlp(x2d, mod6, norm2_w[l], w_up[l].astype(BF16), w_down[l].astype(BF16), final_norm_w,
                  seq, final=(l == depth - 1))
    return x2d.reshape(bsz, seq, d)
```

```python
import functools
import math

import jax
import jax.numpy as jnp
import numpy as np
from jax import lax
from jax.experimental import pallas as pl
from jax.experimental.pallas import tpu as pltpu

F32 = jnp.float32
BF16 = jnp.bfloat16
I32 = jnp.int32

NORM_EPS = 1e-6
HEAD_DIM = 64
ROPE_THETA = 500000.0
LANES = 128

SSD_HEADS = 16
SSD_GROUPS = 2
SSD_STATE = 128
SSD_CHUNK = 128
SSD_HEAD_DIM = 64
SSD_CONV = 4

SB_HEADS = 8
DSA_HEADS = 8
DSA_KV_HEADS = 2
IDX_HEADS = 8
IDX_DIM = 64
DSA_MAX_TOPK = 256

ATT_BLOCK = 256
SB_Q_BLOCK = 512
VMEM_LIMIT = 56 * 1024 * 1024

LOG2E = 1.4426950408889634
INT_MIN = -(2 ** 31)
NEG_BIG = -1e30


def _cparams(sem):
    return pltpu.CompilerParams(dimension_semantics=sem, vmem_limit_bytes=VMEM_LIMIT)


def _dot(a, b):
    return jnp.dot(a, b, preferred_element_type=F32)


def _dot_nt(a, b):
    return lax.dot_general(a, b, (((1,), (1,)), ((), ())), preferred_element_type=F32)


def _softplus(x):
    return jnp.maximum(x, 0.0) + jnp.log1p(jnp.exp(-jnp.abs(x)))


def _norm_mod(x, w, scale, shift):
    y = x * lax.rsqrt(jnp.mean(x * x, axis=-1, keepdims=True) + NORM_EPS)
    return (y * w) * (1.0 + scale) + shift


def _ada_kernel(c_ref, w_ref, b_ref, o_ref):
    c = c_ref[...]
    a = (c * jax.nn.sigmoid(c)).astype(BF16)
    o_ref[0] = _dot(a, w_ref[0].astype(BF16)) + b_ref[0]


def ada_mod(c, ada_w, ada_b, tn=1536):
    depth, d, n = ada_w.shape
    bsz = c.shape[0]
    return pl.pallas_call(
        _ada_kernel,
        out_shape=jax.ShapeDtypeStruct((depth, bsz, n), F32),
        grid=(depth, n // tn),
        in_specs=[
            pl.BlockSpec((bsz, d), lambda l, j: (0, 0)),
            pl.BlockSpec((1, d, tn), lambda l, j: (l, 0, j)),
            pl.BlockSpec((1, 1, tn), lambda l, j: (l, 0, j)),
        ],
        out_specs=pl.BlockSpec((1, bsz, tn), lambda l, j: (l, 0, j)),
        compiler_params=_cparams(("parallel", "parallel")),
        name="ada_mod",
    )(c, ada_w, ada_b.reshape(depth, 1, n))


def _proj_kernel(*refs, splits, rope_chunks, has_rope):
    if has_rope:
        x_ref, mod_ref, nw_ref, w_ref, cos_ref, s1_ref, s2_ref = refs[:7]
        out_refs = refs[7:]
    else:
        x_ref, mod_ref, nw_ref, w_ref = refs[:4]
        out_refs = refs[4:]
    h = _norm_mod(x_ref[...], nw_ref[...], mod_ref[0, 1:2, :], mod_ref[0, 0:1, :]).astype(BF16)
    off = 0
    for o_ref, width in zip(out_refs, splits):
        acc = _dot(h, w_ref[:, off:off + width])
        if has_rope:
            cos, s1, s2 = cos_ref[...], s1_ref[...], s2_ref[...]
            pieces = []
            for c in range(width // LANES):
                xc = acc[:, c * LANES:(c + 1) * LANES]
                if (off // LANES + c) in rope_chunks:
                    xc = (xc * cos + pltpu.roll(xc, LANES - 8, axis=1) * s1
                          + pltpu.roll(xc, 8, axis=1) * s2)
                pieces.append(xc)
            acc = pieces[0] if len(pieces) == 1 else jnp.concatenate(pieces, axis=1)
        o_ref[...] = acc.astype(o_ref.dtype)
        off += width


def proj(x2d, mod6, norm_w, w, splits, dtypes, seq_len, rope=None, rope_chunks=(), tm=512):
    m, d = x2d.shape
    n = w.shape[1]
    assert sum(splits) == n and m % tm == 0 and seq_len % tm == 0
    tiles_per_seq = seq_len // tm
    in_specs = [
        pl.BlockSpec((tm, d), lambda i: (i, 0)),
        pl.BlockSpec((1, 6, d), lambda i: (i // tiles_per_seq, 0, 0)),
        pl.BlockSpec((1, d), lambda i: (0, 0)),
        pl.BlockSpec((d, n), lambda i: (0, 0)),
    ]
    args = [x2d, mod6, norm_w.reshape(1, d), w]
    if rope is not None:
        for t in rope:
            in_specs.append(pl.BlockSpec((tm, LANES), lambda i: (i % tiles_per_seq, 0)))
            args.append(t)
    kern = functools.partial(_proj_kernel, splits=tuple(splits), rope_chunks=tuple(rope_chunks),
                             has_rope=rope is not None)
    return pl.pallas_call(
        kern,
        out_shape=[jax.ShapeDtypeStruct((m, s), dt) for s, dt in zip(splits, dtypes)],
        grid=(m // tm,),
        in_specs=in_specs,
        out_specs=[pl.BlockSpec((tm, s), lambda i: (i, 0)) for s in splits],
        compiler_params=_cparams(("parallel",)),
        name="proj",
    )(*args)


def rope_tables(seq_len):
    rot = HEAD_DIM // 4
    half = rot // 2
    inv_freq = jnp.exp(jnp.arange(half, dtype=F32) * (-2.0 * math.log(ROPE_THETA) / rot))
    ang = jnp.arange(seq_len, dtype=jnp.int32).astype(F32)[:, None] * inv_freq[None, :]
    cos, sin = jnp.cos(ang), jnp.sin(ang)
    ones = jnp.ones((seq_len, HEAD_DIM - rot), F32)
    zeros = jnp.zeros((seq_len, HEAD_DIM - rot), F32)
    zh = jnp.zeros((seq_len, half), F32)
    c64 = jnp.concatenate([cos, cos, ones], axis=1)
    s1_64 = jnp.concatenate([-sin, zh, zeros], axis=1)
    s2_64 = jnp.concatenate([zh, sin, zeros], axis=1)
    tile = lambda a: jnp.concatenate([a, a], axis=1)
    return tile(c64), tile(s1_64), tile(s2_64)


def _expand_heads(a):
    q = a.shape[0]
    lane = lax.broadcasted_iota(I32, (q, LANES), 1)
    chunks = []
    for c in range(SSD_HEADS // 2):
        lo = jnp.broadcast_to(a[:, 2 * c:2 * c + 1], (q, LANES))
        hi = jnp.broadcast_to(a[:, 2 * c + 1:2 * c + 2], (q, LANES))
        chunks.append(jnp.where(lane < SSD_HEAD_DIM, lo, hi))
    return jnp.concatenate(chunks, axis=1)


def _ssd_kernel(z_ref, xbc_ref, dt_ref, cw_ref, cb_ref, dtb_ref, alog_ref, dskip_ref, nw_ref,
                o_ref, state_ref, tail_ref):
    q = SSD_CHUNK
    di = SSD_HEADS * SSD_HEAD_DIM
    gw = SSD_STATE
    hpg = SSD_HEADS // SSD_GROUPS

    @pl.when(pl.program_id(1) == 0)
    def _():
        state_ref[...] = jnp.zeros_like(state_ref)
        tail_ref[...] = jnp.zeros_like(tail_ref)

    cur = xbc_ref[0]
    row = lax.broadcasted_iota(I32, (q, 1), 0)
    tail = tail_ref[...]
    conv = cb_ref[...] + cw_ref[SSD_CONV - 1:SSD_CONV, :] * cur
    for j in range(1, SSD_CONV):
        rolled = pltpu.roll(cur, j, axis=0)
        patch = jnp.tile(pltpu.roll(tail, j, axis=0), (q // 8, 1))
        shifted = jnp.where(row < j, patch, rolled)
        conv = conv + cw_ref[SSD_CONV - 1 - j:SSD_CONV - j, :] * shifted
    tail_ref[...] = cur[q - 8:, :]
    xbc = conv * jax.nn.sigmoid(conv)
    xs = xbc[:, :di]
    bm = xbc[:, di:di + SSD_GROUPS * gw]
    cm = xbc[:, di + SSD_GROUPS * gw:]

    dt = _softplus(dt_ref[0] + dtb_ref[...])
    da = dt * (-jnp.exp(alog_ref[...]))
    acum = da
    s = 1
    while s < q:
        acum = acum + jnp.where(row >= s, pltpu.roll(acum, s, axis=0), 0.0)
        s *= 2
    acum_t = acum.T
    a_last = acum[q - 1:q, :]
    e_in = jnp.exp(acum)
    e_out = jnp.exp(a_last - acum)

    dt_x = _expand_heads(dt)
    e_in_x = _expand_heads(e_in)
    e_out_x = _expand_heads(e_out)
    e_last_x = _expand_heads(jnp.exp(a_last))

    xdt = xs * dt_x
    xdt_b = xdt.astype(BF16)
    xdt_out_b = (xdt * e_out_x).astype(BF16)

    ri = lax.broadcasted_iota(I32, (q, q), 0)
    ci = lax.broadcasted_iota(I32, (q, q), 1)
    causal = ri >= ci

    y_parts = []
    new_states = []
    for g in range(SSD_GROUPS):
        bm_g = bm[:, g * gw:(g + 1) * gw]
        cm_g = cm[:, g * gw:(g + 1) * gw].astype(BF16)
        bm_gb = bm_g.astype(BF16)
        cb = _dot_nt(cm_g, bm_gb)
        sl = slice(g * hpg * SSD_HEAD_DIM, (g + 1) * hpg * SSD_HEAD_DIM)
        st_g = state_ref[:, sl]
        y_off = _dot(cm_g, st_g.astype(BF16)) * e_in_x[:, sl]
        for r in range(hpg):
            h = g * hpg + r
            diff = acum[:, h:h + 1] - acum_t[h:h + 1, :]
            lmat = jnp.exp(jnp.where(causal, diff, NEG_BIG))
            m_h = (cb * lmat).astype(BF16)
            y_parts.append(_dot(m_h, xdt_b[:, h * SSD_HEAD_DIM:(h + 1) * SSD_HEAD_DIM])
                           + y_off[:, r * SSD_HEAD_DIM:(r + 1) * SSD_HEAD_DIM])
        upd = _dot(bm_g.T.astype(BF16), xdt_out_b[:, sl])
        new_states.append(st_g * e_last_x[:, sl] + upd)
    for g in range(SSD_GROUPS):
        sl = slice(g * hpg * SSD_HEAD_DIM, (g + 1) * hpg * SSD_HEAD_DIM)
        state_ref[:, sl] = new_states[g]

    y = jnp.concatenate(y_parts, axis=1) + xs * _expand_heads(dskip_ref[...])
    zg = z_ref[0]
    y = y * (zg * jax.nn.sigmoid(zg))
    y = y * lax.rsqrt(jnp.mean(y * y, axis=-1, keepdims=True) + NORM_EPS) * nw_ref[...]
    o_ref[0] = y.astype(o_ref.dtype)


def ssd_mixer(z, xbc, dt, conv_w, conv_b, dt_bias, a_log, d_skip, norm_w):
    bsz, seq, di = z.shape
    cch = xbc.shape[-1]
    nc = seq // SSD_CHUNK
    pad = lambda v: jnp.pad(v.astype(F32), (0, LANES - v.shape[0])).reshape(1, LANES)
    full = lambda shape: pl.BlockSpec(shape, lambda b, c: (0,) * len(shape))
    return pl.pallas_call(
        _ssd_kernel,
        out_shape=jax.ShapeDtypeStruct((bsz, seq, di), BF16),
        grid=(bsz, nc),
        in_specs=[
            pl.BlockSpec((1, SSD_CHUNK, di), lambda b, c: (b, c, 0)),
            pl.BlockSpec((1, SSD_CHUNK, cch), lambda b, c: (b, c, 0)),
            pl.BlockSpec((1, SSD_CHUNK, LANES), lambda b, c: (b, c, 0)),
            full((SSD_CONV, cch)), full((1, cch)),
            full((1, LANES)), full((1, LANES)), full((1, LANES)), full((1, di)),
        ],
        out_specs=pl.BlockSpec((1, SSD_CHUNK, di), lambda b, c: (b, c, 0)),
        scratch_shapes=[pltpu.VMEM((SSD_STATE, di), F32), pltpu.VMEM((8, cch), F32)],
        compiler_params=_cparams(("parallel", "arbitrary")),
        name="ssd",
    )(z, xbc, dt, conv_w, conv_b.reshape(1, cch), pad(dt_bias), pad(a_log), pad(d_skip),
      norm_w.reshape(1, di))


def _sb_kernel(q_ref, k_ref, v_ref, o_ref, acc_ref, carry_ref):
    tq, tk = SB_Q_BLOCK, ATT_BLOCK
    qi = pl.program_id(2)
    sr = lax.broadcasted_iota(I32, (tk, tk), 0)
    sc = lax.broadcasted_iota(I32, (tk, tk), 1)
    strict = sc < sr
    suffix = jnp.where(sr > sc, 1.0, 0.0).astype(BF16)

    acc_ref[...] = jnp.zeros_like(acc_ref)
    carry_ref[...] = jnp.zeros_like(carry_ref)

    def block(j, r0, nr, masked):
        start = pl.multiple_of(j * tk, tk)
        rows = slice(r0, r0 + nr)
        for hh in range(2):
            lanes = slice(hh * HEAD_DIM, (hh + 1) * HEAD_DIM)
            kj = k_ref[0, pl.ds(start, tk), lanes]
            vj = v_ref[0, pl.ds(start, tk), lanes]
            z = _dot_nt(q_ref[0, rows, lanes], kj) * LOG2E
            sp = jnp.maximum(z, 0.0) + jnp.log2(1.0 + jnp.exp2(-jnp.abs(z)))
            spm = jnp.where(strict, sp, 0.0) if masked else sp
            carry = carry_ref[hh, rows, :]
            after = _dot(spm.astype(BF16), suffix) + jnp.concatenate([carry, carry], axis=1)
            att = jnp.exp2(z - sp - after)
            if masked:
                att = jnp.where(strict, att, 0.0)
            acc_ref[hh, rows, :] += _dot(att.astype(BF16), vj)
            carry_ref[hh, rows, :] = jnp.broadcast_to(after[:, 0:1] + spm[:, 0:1], (nr, LANES))

    nkb = (qi + 1) * (tq // tk)
    block(nkb - 1, tk, tk, True)
    block(nkb - 2, tk, tk, False)
    block(nkb - 2, 0, tk, True)

    def body(i, c):
        for d in range(tq // tk):
            block((qi - i) * (tq // tk) - 1 - d, 0, tq, False)
        return c

    lax.fori_loop(0, qi, body, 0)
    o_ref[0] = jnp.concatenate([acc_ref[0], acc_ref[1]], axis=1).astype(o_ref.dtype)


def sb_attention(qkv, bsz, seq):
    width = SB_HEADS * HEAD_DIM
    npair = width // LANES
    nq = seq // SB_Q_BLOCK
    assert SB_Q_BLOCK == 2 * ATT_BLOCK and seq % SB_Q_BLOCK == 0
    return pl.pallas_call(
        _sb_kernel,
        out_shape=jax.ShapeDtypeStruct((bsz, seq, width), BF16),
        grid=(bsz, npair, nq),
        in_specs=[
            pl.BlockSpec((1, SB_Q_BLOCK, LANES), lambda b, p, i: (b, i, p)),
            pl.BlockSpec((1, seq, LANES), lambda b, p, i: (b, 0, npair + p)),
            pl.BlockSpec((1, seq, LANES), lambda b, p, i: (b, 0, 2 * npair + p)),
        ],
        out_specs=pl.BlockSpec((1, SB_Q_BLOCK, LANES), lambda b, p, i: (b, i, p)),
        scratch_shapes=[pltpu.VMEM((2, SB_Q_BLOCK, HEAD_DIM), F32),
                        pltpu.VMEM((2, SB_Q_BLOCK, LANES), F32)],
        compiler_params=_cparams(("parallel", "parallel", "arbitrary")),
        name="stickbreak",
    )(qkv, qkv, qkv)


def _fold_rows(m):
    parts = [m[i * 8:(i + 1) * 8, :] for i in range(m.shape[0] // 8)]
    while len(parts) > 1:
        parts = [a + b for a, b in zip(parts[0::2], parts[1::2])]
    return parts[0]


def _ordered_to_float(u):
    key = u ^ INT_MIN
    bits = jnp.where(key < 0, key ^ 0x7FFFFFFF, key)
    return pltpu.bitcast(bits, F32)


def _dsa_kernel(dq_ref, dk_ref, dv_ref, iq_ref, ik_ref, iw_ref, o_ref,
                sc_ref, bias_ref, qs_ref, s_ref, mx_ref, l_ref, acc_ref):
    t = ATT_BLOCK
    qi = pl.program_id(1)
    nkb = qi + 1
    rpg = DSA_HEADS // DSA_KV_HEADS
    topk = float(DSA_MAX_TOPK)
    si = lax.broadcasted_iota(I32, (t, t), 0)
    ti = lax.broadcasted_iota(I32, (t, t), 1)

    qs_ref[:, HEAD_DIM:] = jnp.zeros((qs_ref.shape[0], LANES - HEAD_DIM), BF16)
    for h in range(IDX_HEADS):
        qs_ref[h * t:(h + 1) * t, 0:IDX_DIM] = iq_ref[0, :, h * IDX_DIM:(h + 1) * IDX_DIM]
    wt = (iw_ref[0] * (IDX_HEADS ** -0.5)).T

    def score_block(j, c):
        start = pl.multiple_of(j * t, t)
        ikj = ik_ref[0, pl.ds(start, t), :]
        logits = _dot_nt(ikj, qs_ref[...])
        score = jnp.zeros((t, t), F32)
        for h in range(IDX_HEADS):
            score = score + wt[h:h + 1, :] * jnp.maximum(logits[:, h * t:(h + 1) * t], 0.0)
        causal = (j * t + si) <= (qi * t + ti)
        sc_ref[j] = jnp.where(causal, score, -jnp.inf)
        return c

    lax.fori_loop(0, nkb, score_block, 0)

    def count_ge(cand):
        def body(j, cnt):
            return cnt + _fold_rows(jnp.where(sc_ref[j] >= cand, 1.0, 0.0))
        cnt = lax.fori_loop(0, nkb, body, jnp.zeros((8, t), F32))
        return jnp.sum(cnt, axis=0, keepdims=True)

    @pl.when(qi == 0)
    def _():
        bias_ref[0] = jnp.where(ti <= si, 0.0, NEG_BIG)

    @pl.when(qi > 0)
    def _():
        def bit_step(i, c):
            thr_u, cnt_thr = c
            cand_u = thr_u | lax.shift_left(jnp.int32(1), 31 - i)
            cnt = count_ge(_ordered_to_float(cand_u))
            ok = cnt >= topk
            return jnp.where(ok, cand_u, thr_u), jnp.where(ok, cnt, cnt_thr)

        thr_u, cnt_thr = lax.fori_loop(
            0, 32, bit_step, (jnp.zeros((1, t), I32), jnp.full((1, t), topk, F32)))
        thr = _ordered_to_float(thr_u)
        excess = jnp.max(cnt_thr) > topk

        @pl.when(jnp.logical_not(excess))
        def _():
            def fill(j, c):
                bias_ref[j] = jnp.where(sc_ref[j] >= thr, 0.0, NEG_BIG).T
                return c
            lax.fori_loop(0, nkb, fill, 0)

        @pl.when(excess)
        def _():
            nxt = _ordered_to_float(thr_u + 1)
            need = topk - count_ge(nxt)
            incl = jnp.where(ti <= si, 1.0, 0.0).astype(BF16)

            def fix(j, seen):
                sc = sc_ref[j]
                gt = sc >= nxt
                eq = jnp.logical_and(sc >= thr, jnp.logical_not(gt))
                rank = _dot(incl, jnp.where(eq, 1.0, 0.0).astype(BF16)) + seen
                sel = jnp.logical_or(gt, jnp.logical_and(eq, rank <= need))
                bias_ref[j] = jnp.where(sel, 0.0, NEG_BIG).T
                return rank[t - 1:t, :]

            lax.fori_loop(0, nkb, fix, jnp.zeros((1, t), F32))

    for h in range(DSA_HEADS):
        qs_ref[h * t:(h + 1) * t, 0:HEAD_DIM] = dq_ref[0, :, h * HEAD_DIM:(h + 1) * HEAD_DIM]
    mx_ref[...] = jnp.full(mx_ref.shape, NEG_BIG, F32)
    grp = [(slice(g * rpg * t, (g + 1) * rpg * t), slice(g * LANES, (g + 1) * LANES))
           for g in range(DSA_KV_HEADS)]

    def pass_scores(j, c):
        start = pl.multiple_of(j * t, t)
        bias = jnp.concatenate([bias_ref[j]] * rpg, axis=0)
        for rows, lanes in grp:
            kj = dk_ref[0, pl.ds(start, t), lanes]
            s = _dot_nt(qs_ref[rows, :], kj) + bias
            s_ref[j, rows, :] = s
            mx_ref[rows, :] = jnp.maximum(mx_ref[rows, :], jnp.maximum(s[:, :LANES], s[:, LANES:]))
        return c

    lax.fori_loop(0, nkb, pass_scores, 0)
    m = jnp.max(mx_ref[...], axis=1, keepdims=True)
    mx_ref[...] = jnp.broadcast_to(m, mx_ref.shape)
    l_ref[...] = jnp.zeros_like(l_ref)
    acc_ref[...] = jnp.zeros_like(acc_ref)

    def pass_values(j, c):
        start = pl.multiple_of(j * t, t)
        for rows, lanes in grp:
            vj = dv_ref[0, pl.ds(start, t), lanes]
            mb = mx_ref[rows, :]
            p = jnp.exp(s_ref[j, rows, :] - jnp.concatenate([mb, mb], axis=1))
            l_ref[rows, :] += p[:, :LANES] + p[:, LANES:]
            acc_ref[rows, :] += _dot(p.astype(BF16), vj)
        return c

    lax.fori_loop(0, nkb, pass_values, 0)
    out = acc_ref[...] / jnp.sum(l_ref[...], axis=1, keepdims=True)
    for h in range(DSA_HEADS):
        o_ref[0, :, h * HEAD_DIM:(h + 1) * HEAD_DIM] = out[h * t:(h + 1) * t, 0:HEAD_DIM].astype(o_ref.dtype)


def dsa_attention(dq, dk, dv, iq, ik, iw, bsz, seq):
    t = ATT_BLOCK
    nq = seq // t
    assert IDX_HEADS == DSA_HEADS and IDX_DIM == HEAD_DIM
    qblk = lambda w: pl.BlockSpec((1, t, w), lambda b, i: (b, i, 0))
    kblk = lambda w: pl.BlockSpec((1, seq, w), lambda b, i: (b, 0, 0))
    return pl.pallas_call(
        _dsa_kernel,
        out_shape=jax.ShapeDtypeStruct((bsz, seq, DSA_HEADS * HEAD_DIM), BF16),
        grid=(bsz, nq),
        in_specs=[qblk(DSA_HEADS * HEAD_DIM), kblk(DSA_KV_HEADS * LANES), kblk(DSA_KV_HEADS * LANES),
                  qblk(IDX_HEADS * IDX_DIM), kblk(LANES), qblk(LANES)],
        out_specs=qblk(DSA_HEADS * HEAD_DIM),
        scratch_shapes=[
            pltpu.VMEM((nq, t, t), F32),
            pltpu.VMEM((nq, t, t), F32),
            pltpu.VMEM((DSA_HEADS * t, LANES), BF16),
            pltpu.VMEM((nq, DSA_HEADS * t, t), F32),
            pltpu.VMEM((DSA_HEADS * t, LANES), F32),
            pltpu.VMEM((DSA_HEADS * t, LANES), F32),
            pltpu.VMEM((DSA_HEADS * t, LANES), F32),
        ],
        compiler_params=_cparams(("parallel", "arbitrary")),
        name="dsa",
    )(dq, dk, dv, iq, ik, iw)


def _merge_kernel(x_ref, mod_ref, ys_ref, yb_ref, yd_ref, g0_ref, g1_ref, g2_ref,
                  ws_ref, wb_ref, wd_ref, wo_ref, o_ref):
    merged = (jax.nn.sigmoid(g0_ref[...]) * _dot(ys_ref[...], ws_ref[...])
              + jax.nn.sigmoid(g1_ref[...]) * _dot(yb_ref[...], wb_ref[...])
              + jax.nn.sigmoid(g2_ref[...]) * _dot(yd_ref[...], wd_ref[...]))
    o_ref[...] = x_ref[...] + mod_ref[0, 2:3, :] * _dot(merged.astype(BF16), wo_ref[...])


def merge(x2d, mod6, y_ssd, y_sb, y_dsa, gates, w_ssd, w_sb, w_dsa, w_out, seq_len, tm=512):
    m, d = x2d.shape
    tiles_per_seq = seq_len // tm
    row = lambda w: pl.BlockSpec((tm, w), lambda i: (i, 0))
    full = lambda a: pl.BlockSpec(a.shape, lambda i: (0, 0))
    return pl.pallas_call(
        _merge_kernel,
        out_shape=jax.ShapeDtypeStruct((m, d), F32),
        grid=(m // tm,),
        in_specs=[
            row(d),
            pl.BlockSpec((1, 6, d), lambda i: (i // tiles_per_seq, 0, 0)),
            row(y_ssd.shape[1]), row(y_sb.shape[1]), row(y_dsa.shape[1]),
            pl.BlockSpec((tm, d), lambda i: (i, 0)),
            pl.BlockSpec((tm, d), lambda i: (i, 1)),
            pl.BlockSpec((tm, d), lambda i: (i, 2)),
            full(w_ssd), full(w_sb), full(w_dsa), full(w_out),
        ],
        out_specs=row(d),
        compiler_params=_cparams(("parallel",)),
        name="merge",
    )(x2d, mod6, y_ssd, y_sb, y_dsa, gates, gates, gates, w_ssd, w_sb, w_dsa, w_out)


def _mlp_kernel(x_ref, mod_ref, nw_ref, wu_ref, wd_ref, fw_ref, o_ref, h_ref, acc_ref, *, final):
    j = pl.program_id(1)

    @pl.when(j == 0)
    def _():
        h_ref[...] = _norm_mod(x_ref[...], nw_ref[...], mod_ref[0, 4:5, :],
                               mod_ref[0, 3:4, :]).astype(BF16)
        acc_ref[...] = jnp.zeros_like(acc_ref)

    u = jnp.maximum(_dot(h_ref[...], wu_ref[...]), 0.0)
    acc_ref[...] += _dot((u * u).astype(BF16), wd_ref[...])

    @pl.when(j == pl.num_programs(1) - 1)
    def _():
        y = x_ref[...] + mod_ref[0, 5:6, :] * acc_ref[...]
        if final:
            y = y * lax.rsqrt(jnp.mean(y * y, axis=-1, keepdims=True) + NORM_EPS) * fw_ref[...]
        o_ref[...] = y


def mlp(x2d, mod6, norm_w, w_up, w_down, final_w, seq_len, final, tm=1024, th=1024):
    m, d = x2d.shape
    hid = w_up.shape[1]
    tiles_per_seq = seq_len // tm
    return pl.pallas_call(
        functools.partial(_mlp_kernel, final=final),
        out_shape=jax.ShapeDtypeStruct((m, d), F32),
        grid=(m // tm, hid // th),
        in_specs=[
            pl.BlockSpec((tm, d), lambda i, j: (i, 0)),
            pl.BlockSpec((1, 6, d), lambda i, j: (i // tiles_per_seq, 0, 0)),
            pl.BlockSpec((1, d), lambda i, j: (0, 0)),
            pl.BlockSpec((d, th), lambda i, j: (0, j)),
            pl.BlockSpec((th, d), lambda i, j: (j, 0)),
            pl.BlockSpec((1, d), lambda i, j: (0, 0)),
        ],
        out_specs=pl.BlockSpec((tm, d), lambda i, j: (i, 0)),
        scratch_shapes=[pltpu.VMEM((tm, d), BF16), pltpu.VMEM((tm, d), F32)],
        compiler_params=_cparams(("parallel", "arbitrary")),
        name="mlp",
    )(x2d, mod6, norm_w.reshape(1, d), w_up, w_down, final_w.reshape(1, d))


def _split_w_in(w):
    d = w.shape[0]
    di = SSD_HEADS * SSD_HEAD_DIM
    cch = di + 2 * SSD_GROUPS * SSD_STATE
    sbw = SB_HEADS * HEAD_DIM
    dsw = DSA_HEADS * HEAD_DIM
    kvw = DSA_KV_HEADS * HEAD_DIM
    sizes = [di, cch, SSD_HEADS, sbw, sbw, sbw, dsw, kvw, kvw, IDX_HEADS * IDX_DIM, IDX_DIM, IDX_HEADS]
    offs = np.cumsum([0] + sizes)
    seg = [w[:, offs[i]:offs[i + 1]] for i in range(len(sizes))]
    gate = w[:, offs[-1]:]
    z_w, xbc_w, dt_w, sq, sk, sv, dq, dk, dv, iq, ik, iw = seg
    zpad = lambda a, n: jnp.pad(a, ((0, 0), (0, n - a.shape[1])))
    scale = HEAD_DIM ** -0.5
    w_ssd = jnp.concatenate([z_w, xbc_w, zpad(dt_w, LANES)], axis=1)
    w_sb = jnp.concatenate([sq * scale, sk, sv], axis=1)
    kv_pad = lambda a: jnp.concatenate(
        [zpad(a[:, g * HEAD_DIM:(g + 1) * HEAD_DIM], LANES) for g in range(DSA_KV_HEADS)], axis=1)
    w_dsa = jnp.concatenate([dq * scale, kv_pad(dk), kv_pad(dv), iq * (IDX_DIM ** -0.5),
                             zpad(ik, LANES), zpad(iw, LANES)], axis=1)
    return (w_ssd.astype(BF16), w_sb.astype(BF16), w_dsa.astype(BF16), gate.astype(BF16))


def kernel(x, c, norm1_w, ada_w, ada_b, w_in, conv_w, conv_b, dt_bias, a_log, d_skip, ssd_norm_w,
           w_br_ssd, w_br_sb, w_br_dsa, w_out, norm2_w, w_up, w_down, final_norm_w):
    bsz, seq, d = x.shape
    depth = ada_w.shape[0]
    m = bsz * seq
    di = SSD_HEADS * SSD_HEAD_DIM
    cch = di + 2 * SSD_GROUPS * SSD_STATE
    dsw = DSA_HEADS * HEAD_DIM
    kvl = DSA_KV_HEADS * LANES
    rope = rope_tables(seq)
    mods = ada_mod(c, ada_w, ada_b).reshape(depth, bsz, 6, d)
    x2d = x.reshape(m, d)
    for l in range(depth):
        mod6 = mods[l]
        w_ssd, w_sb, w_dsa, w_gate = _split_w_in(w_in[l])
        z, xbc, dt = proj(x2d, mod6, norm1_w[l], w_ssd, (di, cch, LANES), (F32, F32, F32), seq)
        (qkv,) = proj(x2d, mod6, norm1_w[l], w_sb, (3 * SB_HEADS * HEAD_DIM,), (BF16,), seq)
        dq, dk, dv, iq, ik, iw = proj(
            x2d, mod6, norm1_w[l], w_dsa, (dsw, kvl, kvl, dsw, LANES, LANES),
            (BF16, BF16, BF16, BF16, BF16, F32), seq, rope=rope,
            rope_chunks=(0, 1, 2, 3, 4, 5, 8, 9, 10, 11, 12))
        (gates,) = proj(x2d, mod6, norm1_w[l], w_gate, (3 * d,), (F32,), seq)

        r3 = lambda a: a.reshape(bsz, seq, a.shape[-1])
        y_ssd = ssd_mixer(r3(z), r3(xbc), r3(dt), conv_w[l], conv_b[l], dt_bias[l], a_log[l],
                          d_skip[l], ssd_norm_w[l])
        y_sb = sb_attention(r3(qkv), bsz, seq)
        y_dsa = dsa_attention(r3(dq), r3(dk), r3(dv), r3(iq), r3(ik), r3(iw), bsz, seq)

        x2d = merge(x2d, mod6, y_ssd.reshape(m, di), y_sb.reshape(m, -1), y_dsa.reshape(m, -1), gates,
                    w_br_ssd[l].astype(BF16), w_br_sb[l].astype(BF16), w_br_dsa[l].astype(BF16),
                    w_out[l].astype(BF16), seq)
        x2d = mlp(x2d, mod6, norm2_w[l], w_up[l].astype(BF16), w_down[l].astype(BF16), final_norm_w,
                  seq, final=(l == depth - 1))
    return x2d.reshape(bsz, seq, d)
```

```python
import functools
import math

import jax
import jax.numpy as jnp
import numpy as np
from jax import lax
from jax.experimental import pallas as pl
from jax.experimental.pallas import tpu as pltpu

F32 = jnp.float32
BF16 = jnp.bfloat16
I32 = jnp.int32

NORM_EPS = 1e-6
HEAD_DIM = 64
ROPE_THETA = 500000.0
LANES = 128

SSD_HEADS = 16
SSD_GROUPS = 2
SSD_STATE = 128
SSD_CHUNK = 128
SSD_HEAD_DIM = 64
SSD_CONV = 4

SB_HEADS = 8
DSA_HEADS = 8
DSA_KV_HEADS = 2
IDX_HEADS = 8
IDX_DIM = 64
DSA_MAX_TOPK = 256

ATT_BLOCK = 256
SB_Q_BLOCK = 512
VMEM_LIMIT = 56 * 1024 * 1024

LOG2E = 1.4426950408889634
INT_MIN = -(2 ** 31)
NEG_BIG = -1e30


def _cparams(sem):
    return pltpu.CompilerParams(dimension_semantics=sem, vmem_limit_bytes=VMEM_LIMIT)


def _dot(a, b):
    return jnp.dot(a, b, preferred_element_type=F32)


def _dot_nt(a, b):
    return lax.dot_general(a, b, (((1,), (1,)), ((), ())), preferred_element_type=F32)


def _softplus(x):
    return jnp.maximum(x, 0.0) + jnp.log1p(jnp.exp(-jnp.abs(x)))


def _norm_mod(x, w, scale, shift):
    y = x * lax.rsqrt(jnp.mean(x * x, axis=-1, keepdims=True) + NORM_EPS)
    return (y * w) * (1.0 + scale) + shift


def _ada_kernel(c_ref, w_ref, b_ref, o_ref):
    c = c_ref[...]
    a = (c * jax.nn.sigmoid(c)).astype(BF16)
    o_ref[0] = _dot(a, w_ref[0].astype(BF16)) + b_ref[0]


def ada_mod(c, ada_w, ada_b, tn=1536):
    depth, d, n = ada_w.shape
    bsz = c.shape[0]
    return pl.pallas_call(
        _ada_kernel,
        out_shape=jax.ShapeDtypeStruct((depth, bsz, n), F32),
        grid=(depth, n // tn),
        in_specs=[
            pl.BlockSpec((bsz, d), lambda l, j: (0, 0)),
            pl.BlockSpec((1, d, tn), lambda l, j: (l, 0, j)),
            pl.BlockSpec((1, 1, tn), lambda l, j: (l, 0, j)),
        ],
        out_specs=pl.BlockSpec((1, bsz, tn), lambda l, j: (l, 0, j)),
        compiler_params=_cparams(("parallel", "parallel")),
        name="ada_mod",
    )(c, ada_w, ada_b.reshape(depth, 1, n))


def _proj_kernel(*refs, splits, rope_chunks, has_rope):
    if has_rope:
        x_ref, mod_ref, nw_ref, w_ref, cos_ref, s1_ref, s2_ref = refs[:7]
        out_refs = refs[7:]
    else:
        x_ref, mod_ref, nw_ref, w_ref = refs[:4]
        out_refs = refs[4:]
    h = _norm_mod(x_ref[...], nw_ref[...], mod_ref[0, 1:2, :], mod_ref[0, 0:1, :]).astype(BF16)
    off = 0
    for o_ref, width in zip(out_refs, splits):
        acc = _dot(h, w_ref[:, off:off + width])
        if has_rope:
            cos, s1, s2 = cos_ref[...], s1_ref[...], s2_ref[...]
            pieces = []
            for c in range(width // LANES):
                xc = acc[:, c * LANES:(c + 1) * LANES]
                if (off // LANES + c) in rope_chunks:
                    xc = (xc * cos + pltpu.roll(xc, LANES - 8, axis=1) * s1
                          + pltpu.roll(xc, 8, axis=1) * s2)
                pieces.append(xc)
            acc = pieces[0] if len(pieces) == 1 else jnp.concatenate(pieces, axis=1)
        o_ref[...] = acc.astype(o_ref.dtype)
        off += width


def proj(x2d, mod6, norm_w, w, splits, dtypes, seq_len, rope=None, rope_chunks=(), tm=512):
    m, d = x2d.shape
    n = w.shape[1]
    assert sum(splits) == n and m % tm == 0 and seq_len % tm == 0
    tiles_per_seq = seq_len // tm
    in_specs = [
        pl.BlockSpec((tm, d), lambda i: (i, 0)),
        pl.BlockSpec((1, 6, d), lambda i: (i // tiles_per_seq, 0, 0)),
        pl.BlockSpec((1, d), lambda i: (0, 0)),
        pl.BlockSpec((d, n), lambda i: (0, 0)),
    ]
    args = [x2d, mod6, norm_w.reshape(1, d), w]
    if rope is not None:
        for t in rope:
            in_specs.append(pl.BlockSpec((tm, LANES), lambda i: (i % tiles_per_seq, 0)))
            args.append(t)
    kern = functools.partial(_proj_kernel, splits=tuple(splits), rope_chunks=tuple(rope_chunks),
                             has_rope=rope is not None)
    return pl.pallas_call(
        kern,
        out_shape=[jax.ShapeDtypeStruct((m, s), dt) for s, dt in zip(splits, dtypes)],
        grid=(m // tm,),
        in_specs=in_specs,
        out_specs=[pl.BlockSpec((tm, s), lambda i: (i, 0)) for s in splits],
        compiler_params=_cparams(("parallel",)),
        name="proj",
    )(*args)


def rope_tables(seq_len):
    rot = HEAD_DIM // 4
    half = rot // 2
    inv_freq = jnp.exp(jnp.arange(half, dtype=F32) * (-2.0 * math.log(ROPE_THETA) / rot))
    ang = jnp.arange(seq_len, dtype=jnp.int32).astype(F32)[:, None] * inv_freq[None, :]
    cos, sin = jnp.cos(ang), jnp.sin(ang)
    ones = jnp.ones((seq_len, HEAD_DIM - rot), F32)
    zeros = jnp.zeros((seq_len, HEAD_DIM - rot), F32)
    zh = jnp.zeros((seq_len, half), F32)
    c64 = jnp.concatenate([cos, cos, ones], axis=1)
    s1_64 = jnp.concatenate([-sin, zh, zeros], axis=1)
    s2_64 = jnp.concatenate([zh, sin, zeros], axis=1)
    tile = lambda a: jnp.concatenate([a, a], axis=1)
    return tile(c64), tile(s1_64), tile(s2_64)


def _expand_heads(a):
    q = a.shape[0]
    lane = lax.broadcasted_iota(I32, (q, LANES), 1)
    chunks = []
    for c in range(SSD_HEADS // 2):
        lo = jnp.broadcast_to(a[:, 2 * c:2 * c + 1], (q, LANES))
        hi = jnp.broadcast_to(a[:, 2 * c + 1:2 * c + 2], (q, LANES))
        chunks.append(jnp.where(lane < SSD_HEAD_DIM, lo, hi))
    return jnp.concatenate(chunks, axis=1)


def _ssd_kernel(z_ref, xbc_ref, dt_ref, cw_ref, cb_ref, dtb_ref, alog_ref, dskip_ref, nw_ref,
                o_ref, state_ref, tail_ref):
    q = SSD_CHUNK
    di = SSD_HEADS * SSD_HEAD_DIM
    gw = SSD_STATE
    hpg = SSD_HEADS // SSD_GROUPS

    @pl.when(pl.program_id(1) == 0)
    def _():
        state_ref[...] = jnp.zeros_like(state_ref)
        tail_ref[...] = jnp.zeros_like(tail_ref)

    cur = xbc_ref[0]
    row = lax.broadcasted_iota(I32, (q, 1), 0)
    tail = tail_ref[...]
    conv = cb_ref[...] + cw_ref[SSD_CONV - 1:SSD_CONV, :] * cur
    for j in range(1, SSD_CONV):
        rolled = pltpu.roll(cur, j, axis=0)
        patch = jnp.tile(pltpu.roll(tail, j, axis=0), (q // 8, 1))
        shifted = jnp.where(row < j, patch, rolled)
        conv = conv + cw_ref[SSD_CONV - 1 - j:SSD_CONV - j, :] * shifted
    tail_ref[...] = cur[q - 8:, :]
    xbc = conv * jax.nn.sigmoid(conv)
    xs = xbc[:, :di]
    bm = xbc[:, di:di + SSD_GROUPS * gw]
    cm = xbc[:, di + SSD_GROUPS * gw:]

    dt = _softplus(dt_ref[0] + dtb_ref[...])
    da = dt * (-jnp.exp(alog_ref[...]))
    acum = da
    s = 1
    while s < q:
        acum = acum + jnp.where(row >= s, pltpu.roll(acum, s, axis=0), 0.0)
        s *= 2
    acum_t = acum.T
    a_last = acum[q - 1:q, :]
    e_in = jnp.exp(acum)
    e_out = jnp.exp(a_last - acum)

    dt_x = _expand_heads(dt)
    e_in_x = _expand_heads(e_in)
    e_out_x = _expand_heads(e_out)
    e_last_x = _expand_heads(jnp.exp(a_last))

    xdt = xs * dt_x
    xdt_b = xdt.astype(BF16)
    xdt_out_b = (xdt * e_out_x).astype(BF16)

    ri = lax.broadcasted_iota(I32, (q, q), 0)
    ci = lax.broadcasted_iota(I32, (q, q), 1)
    causal = ri >= ci

    y_parts = []
    new_states = []
    for g in range(SSD_GROUPS):
        bm_g = bm[:, g * gw:(g + 1) * gw]
        cm_g = cm[:, g * gw:(g + 1) * gw].astype(BF16)
        bm_gb = bm_g.astype(BF16)
        cb = _dot_nt(cm_g, bm_gb)
        sl = slice(g * hpg * SSD_HEAD_DIM, (g + 1) * hpg * SSD_HEAD_DIM)
        st_g = state_ref[:, sl]
        y_off = _dot(cm_g, st_g.astype(BF16)) * e_in_x[:, sl]
        for r in range(hpg):
            h = g * hpg + r
            diff = acum[:, h:h + 1] - acum_t[h:h + 1, :]
            lmat = jnp.exp(jnp.where(causal, diff, NEG_BIG))
            m_h = (cb * lmat).astype(BF16)
            y_parts.append(_dot(m_h, xdt_b[:, h * SSD_HEAD_DIM:(h + 1) * SSD_HEAD_DIM])
                           + y_off[:, r * SSD_HEAD_DIM:(r + 1) * SSD_HEAD_DIM])
        upd = _dot(bm_g.T.astype(BF16), xdt_out_b[:, sl])
        new_states.append(st_g * e_last_x[:, sl] + upd)
    for g in range(SSD_GROUPS):
        sl = slice(g * hpg * SSD_HEAD_DIM, (g + 1) * hpg * SSD_HEAD_DIM)
        state_ref[:, sl] = new_states[g]

    y = jnp.concatenate(y_parts, axis=1) + xs * _expand_heads(dskip_ref[...])
    zg = z_ref[0]
    y = y * (zg * jax.nn.sigmoid(zg))
    y = y * lax.rsqrt(jnp.mean(y * y, axis=-1, keepdims=True) + NORM_EPS) * nw_ref[...]
    o_ref[0] = y.astype(o_ref.dtype)


def ssd_mixer(z, xbc, dt, conv_w, conv_b, dt_bias, a_log, d_skip, norm_w):
    bsz, seq, di = z.shape
    cch = xbc.shape[-1]
    nc = seq // SSD_CHUNK
    pad = lambda v: jnp.pad(v.astype(F32), (0, LANES - v.shape[0])).reshape(1, LANES)
    full = lambda shape: pl.BlockSpec(shape, lambda b, c: (0,) * len(shape))
    return pl.pallas_call(
        _ssd_kernel,
        out_shape=jax.ShapeDtypeStruct((bsz, seq, di), BF16),
        grid=(bsz, nc),
        in_specs=[
            pl.BlockSpec((1, SSD_CHUNK, di), lambda b, c: (b, c, 0)),
            pl.BlockSpec((1, SSD_CHUNK, cch), lambda b, c: (b, c, 0)),
            pl.BlockSpec((1, SSD_CHUNK, LANES), lambda b, c: (b, c, 0)),
            full((SSD_CONV, cch)), full((1, cch)),
            full((1, LANES)), full((1, LANES)), full((1, LANES)), full((1, di)),
        ],
        out_specs=pl.BlockSpec((1, SSD_CHUNK, di), lambda b, c: (b, c, 0)),
        scratch_shapes=[pltpu.VMEM((SSD_STATE, di), F32), pltpu.VMEM((8, cch), F32)],
        compiler_params=_cparams(("parallel", "arbitrary")),
        name="ssd",
    )(z, xbc, dt, conv_w, conv_b.reshape(1, cch), pad(dt_bias), pad(a_log), pad(d_skip),
      norm_w.reshape(1, di))


def _sb_kernel(q_ref, k_ref, v_ref, o_ref, acc_ref, carry_ref):
    tq, tk = SB_Q_BLOCK, ATT_BLOCK
    qi = pl.program_id(2)
    sr = lax.broadcasted_iota(I32, (tk, tk), 0)
    sc = lax.broadcasted_iota(I32, (tk, tk), 1)
    strict_late = sc < sr
    fr = lax.broadcasted_iota(I32, (tq, tk), 0)
    fc = lax.broadcasted_iota(I32, (tq, tk), 1)
    strict_early = jnp.logical_or(fc < fr, fr >= tk)
    suffix = jnp.where(sr > sc, 1.0, 0.0).astype(BF16)

    acc_ref[...] = jnp.zeros_like(acc_ref)
    carry_ref[...] = jnp.zeros_like(carry_ref)

    def block(j, r0, nr, strict):
        masked = strict is not None
        start = pl.multiple_of(j * tk, tk)
        rows = slice(r0, r0 + nr)
        for hh in range(2):
            lanes = slice(hh * HEAD_DIM, (hh + 1) * HEAD_DIM)
            kj = k_ref[0, pl.ds(start, tk), lanes]
            vj = v_ref[0, pl.ds(start, tk), lanes]
            z = _dot_nt(q_ref[0, rows, lanes], kj) * LOG2E
            sp = jnp.maximum(z, 0.0) + jnp.log2(1.0 + jnp.exp2(-jnp.abs(z)))
            spm = jnp.where(strict, sp, 0.0) if masked else sp
            carry = carry_ref[hh, rows, :]
            after = _dot(spm.astype(BF16), suffix) + jnp.concatenate([carry, carry], axis=1)
            att = jnp.exp2(z - sp - after)
            if masked:
                att = jnp.where(strict, att, 0.0)
            acc_ref[hh, rows, :] += _dot(att.astype(BF16), vj)
            carry_ref[hh, rows, :] = jnp.broadcast_to(after[:, 0:1] + spm[:, 0:1], (nr, LANES))

    nkb = (qi + 1) * (tq // tk)
    block(nkb - 1, tk, tk, strict_late)
    block(nkb - 2, 0, tq, strict_early)

    def body(i, c):
        for d in range(tq // tk):
            block((qi - i) * (tq // tk) - 1 - d, 0, tq, None)
        return c

    lax.fori_loop(0, qi, body, 0)
    o_ref[0] = jnp.concatenate([acc_ref[0], acc_ref[1]], axis=1).astype(o_ref.dtype)


def sb_attention(qkv, bsz, seq):
    width = SB_HEADS * HEAD_DIM
    npair = width // LANES
    nq = seq // SB_Q_BLOCK
    assert SB_Q_BLOCK == 2 * ATT_BLOCK and seq % SB_Q_BLOCK == 0
    return pl.pallas_call(
        _sb_kernel,
        out_shape=jax.ShapeDtypeStruct((bsz, seq, width), BF16),
        grid=(bsz, npair, nq),
        in_specs=[
            pl.BlockSpec((1, SB_Q_BLOCK, LANES), lambda b, p, i: (b, i, p)),
            pl.BlockSpec((1, seq, LANES), lambda b, p, i: (b, 0, npair + p)),
            pl.BlockSpec((1, seq, LANES), lambda b, p, i: (b, 0, 2 * npair + p)),
        ],
        out_specs=pl.BlockSpec((1, SB_Q_BLOCK, LANES), lambda b, p, i: (b, i, p)),
        scratch_shapes=[pltpu.VMEM((2, SB_Q_BLOCK, HEAD_DIM), F32),
                        pltpu.VMEM((2, SB_Q_BLOCK, LANES), F32)],
        compiler_params=_cparams(("parallel", "parallel", "arbitrary")),
        name="stickbreak",
    )(qkv, qkv, qkv)


def _fold_rows(m):
    parts = [m[i * 8:(i + 1) * 8, :] for i in range(m.shape[0] // 8)]
    while len(parts) > 1:
        parts = [a + b for a, b in zip(parts[0::2], parts[1::2])]
    return parts[0]


def _ordered_to_float(u):
    key = u ^ INT_MIN
    bits = jnp.where(key < 0, key ^ 0x7FFFFFFF, key)
    return pltpu.bitcast(bits, F32)


def _dsa_kernel(dq_ref, dk_ref, dv_ref, iq_ref, ik_ref, iw_ref, o_ref,
                sc_ref, bias_ref, qs_ref, s_ref, mx_ref, l_ref, acc_ref):
    t = ATT_BLOCK
    qi = pl.program_id(1)
    nkb = qi + 1
    rpg = DSA_HEADS // DSA_KV_HEADS
    topk = float(DSA_MAX_TOPK)
    si = lax.broadcasted_iota(I32, (t, t), 0)
    ti = lax.broadcasted_iota(I32, (t, t), 1)

    qs_ref[:, HEAD_DIM:] = jnp.zeros((qs_ref.shape[0], LANES - HEAD_DIM), BF16)
    for h in range(IDX_HEADS):
        qs_ref[h * t:(h + 1) * t, 0:IDX_DIM] = iq_ref[0, :, h * IDX_DIM:(h + 1) * IDX_DIM]
    wt = (iw_ref[0] * (IDX_HEADS ** -0.5)).T

    def score_block(j, c):
        start = pl.multiple_of(j * t, t)
        ikj = ik_ref[0, pl.ds(start, t), :]
        logits = _dot_nt(ikj, qs_ref[...])
        score = jnp.zeros((t, t), F32)
        for h in range(IDX_HEADS):
            score = score + wt[h:h + 1, :] * jnp.maximum(logits[:, h * t:(h + 1) * t], 0.0)
        causal = (j * t + si) <= (qi * t + ti)
        sc_ref[j] = jnp.where(causal, score, -jnp.inf)
        return c

    lax.fori_loop(0, nkb, score_block, 0)

    def count_ge(cand):
        def body(j, cnt):
            return cnt + _fold_rows(jnp.where(sc_ref[j] >= cand, 1.0, 0.0))
        cnt = lax.fori_loop(0, nkb, body, jnp.zeros((8, t), F32))
        return jnp.sum(cnt, axis=0, keepdims=True)

    @pl.when(qi == 0)
    def _():
        bias_ref[0] = jnp.where(ti <= si, 0.0, NEG_BIG)

    @pl.when(qi > 0)
    def _():
        def bit_step(i, c):
            thr_u, cnt_thr = c
            cand_u = thr_u | lax.shift_left(jnp.int32(1), 31 - i)
            cnt = count_ge(_ordered_to_float(cand_u))
            ok = cnt >= topk
            return jnp.where(ok, cand_u, thr_u), jnp.where(ok, cnt, cnt_thr)

        thr_u, cnt_thr = lax.fori_loop(
            0, 32, bit_step, (jnp.zeros((1, t), I32), jnp.full((1, t), topk, F32)))
        thr = _ordered_to_float(thr_u)
        excess = jnp.max(cnt_thr) > topk

        @pl.when(jnp.logical_not(excess))
        def _():
            def fill(j, c):
                bias_ref[j] = jnp.where(sc_ref[j] >= thr, 0.0, NEG_BIG).T
                return c
            lax.fori_loop(0, nkb, fill, 0)

        @pl.when(excess)
        def _():
            nxt = _ordered_to_float(thr_u + 1)
            need = topk - count_ge(nxt)
            incl = jnp.where(ti <= si, 1.0, 0.0).astype(BF16)

            def fix(j, seen):
                sc = sc_ref[j]
                gt = sc >= nxt
                eq = jnp.logical_and(sc >= thr, jnp.logical_not(gt))
                rank = _dot(incl, jnp.where(eq, 1.0, 0.0).astype(BF16)) + seen
                sel = jnp.logical_or(gt, jnp.logical_and(eq, rank <= need))
                bias_ref[j] = jnp.where(sel, 0.0, NEG_BIG).T
                return rank[t - 1:t, :]

            lax.fori_loop(0, nkb, fix, jnp.zeros((1, t), F32))

    for h in range(DSA_HEADS):
        qs_ref[h * t:(h + 1) * t, 0:HEAD_DIM] = dq_ref[0, :, h * HEAD_DIM:(h + 1) * HEAD_DIM]
    mx_ref[...] = jnp.full(mx_ref.shape, NEG_BIG, F32)
    grp = [(slice(g * rpg * t, (g + 1) * rpg * t), slice(g * LANES, (g + 1) * LANES))
           for g in range(DSA_KV_HEADS)]

    def pass_scores(j, c):
        start = pl.multiple_of(j * t, t)
        bias = jnp.concatenate([bias_ref[j]] * rpg, axis=0)
        for rows, lanes in grp:
            kj = dk_ref[0, pl.ds(start, t), lanes]
            s = _dot_nt(qs_ref[rows, :], kj) + bias
            s_ref[j, rows, :] = s
            mx_ref[rows, :] = jnp.maximum(mx_ref[rows, :], jnp.maximum(s[:, :LANES], s[:, LANES:]))
        return c

    lax.fori_loop(0, nkb, pass_scores, 0)
    m = jnp.max(mx_ref[...], axis=1, keepdims=True)
    mx_ref[...] = jnp.broadcast_to(m, mx_ref.shape)
    l_ref[...] = jnp.zeros_like(l_ref)
    acc_ref[...] = jnp.zeros_like(acc_ref)

    def pass_values(j, c):
        start = pl.multiple_of(j * t, t)
        for rows, lanes in grp:
            vj = dv_ref[0, pl.ds(start, t), lanes]
            mb = mx_ref[rows, :]
            p = jnp.exp(s_ref[j, rows, :] - jnp.concatenate([mb, mb], axis=1))
            l_ref[rows, :] += p[:, :LANES] + p[:, LANES:]
            acc_ref[rows, :] += _dot(p.astype(BF16), vj)
        return c

    lax.fori_loop(0, nkb, pass_values, 0)
    out = acc_ref[...] / jnp.sum(l_ref[...], axis=1, keepdims=True)
    for h in range(DSA_HEADS):
        o_ref[0, :, h * HEAD_DIM:(h + 1) * HEAD_DIM] = out[h * t:(h + 1) * t, 0:HEAD_DIM].astype(o_ref.dtype)


def dsa_attention(dq, dk, dv, iq, ik, iw, bsz, seq):
    t = ATT_BLOCK
    nq = seq // t
    assert IDX_HEADS == DSA_HEADS and IDX_DIM == HEAD_DIM
    qblk = lambda w: pl.BlockSpec((1, t, w), lambda b, i: (b, i, 0))
    kblk = lambda w: pl.BlockSpec((1, seq, w), lambda b, i: (b, 0, 0))
    return pl.pallas_call(
        _dsa_kernel,
        out_shape=jax.ShapeDtypeStruct((bsz, seq, DSA_HEADS * HEAD_DIM), BF16),
        grid=(bsz, nq),
        in_specs=[qblk(DSA_HEADS * HEAD_DIM), kblk(DSA_KV_HEADS * LANES), kblk(DSA_KV_HEADS * LANES),
                  qblk(IDX_HEADS * IDX_DIM), kblk(LANES), qblk(LANES)],
        out_specs=qblk(DSA_HEADS * HEAD_DIM),
        scratch_shapes=[
            pltpu.VMEM((nq, t, t), F32),
            pltpu.VMEM((nq, t, t), F32),
            pltpu.VMEM((DSA_HEADS * t, LANES), BF16),
            pltpu.VMEM((nq, DSA_HEADS * t, t), F32),
            pltpu.VMEM((DSA_HEADS * t, LANES), F32),
            pltpu.VMEM((DSA_HEADS * t, LANES), F32),
            pltpu.VMEM((DSA_HEADS * t, LANES), F32),
        ],
        compiler_params=_cparams(("parallel", "arbitrary")),
        name="dsa",
    )(dq, dk, dv, iq, ik, iw)


def _merge_kernel(x_ref, mod_ref, ys_ref, yb_ref, yd_ref, g0_ref, g1_ref, g2_ref,
                  ws_ref, wb_ref, wd_ref, wo_ref, o_ref):
    merged = (jax.nn.sigmoid(g0_ref[...]) * _dot(ys_ref[...], ws_ref[...])
              + jax.nn.sigmoid(g1_ref[...]) * _dot(yb_ref[...], wb_ref[...])
              + jax.nn.sigmoid(g2_ref[...]) * _dot(yd_ref[...], wd_ref[...]))
    o_ref[...] = x_ref[...] + mod_ref[0, 2:3, :] * _dot(merged.astype(BF16), wo_ref[...])


def merge(x2d, mod6, y_ssd, y_sb, y_dsa, gates, w_ssd, w_sb, w_dsa, w_out, seq_len, tm=512):
    m, d = x2d.shape
    tiles_per_seq = seq_len // tm
    row = lambda w: pl.BlockSpec((tm, w), lambda i: (i, 0))
    full = lambda a: pl.BlockSpec(a.shape, lambda i: (0, 0))
    return pl.pallas_call(
        _merge_kernel,
        out_shape=jax.ShapeDtypeStruct((m, d), F32),
        grid=(m // tm,),
        in_specs=[
            row(d),
            pl.BlockSpec((1, 6, d), lambda i: (i // tiles_per_seq, 0, 0)),
            row(y_ssd.shape[1]), row(y_sb.shape[1]), row(y_dsa.shape[1]),
            pl.BlockSpec((tm, d), lambda i: (i, 0)),
            pl.BlockSpec((tm, d), lambda i: (i, 1)),
            pl.BlockSpec((tm, d), lambda i: (i, 2)),
            full(w_ssd), full(w_sb), full(w_dsa), full(w_out),
        ],
        out_specs=row(d),
        compiler_params=_cparams(("parallel",)),
        name="merge",
    )(x2d, mod6, y_ssd, y_sb, y_dsa, gates, gates, gates, w_ssd, w_sb, w_dsa, w_out)


def _mlp_kernel(x_ref, mod_ref, nw_ref, wu_ref, wd_ref, fw_ref, o_ref, h_ref, acc_ref, *, final):
    j = pl.program_id(1)

    @pl.when(j == 0)
    def _():
        h_ref[...] = _norm_mod(x_ref[...], nw_ref[...], mod_ref[0, 4:5, :],
                               mod_ref[0, 3:4, :]).astype(BF16)
        acc_ref[...] = jnp.zeros_like(acc_ref)

    u = jnp.maximum(_dot(h_ref[...], wu_ref[...]), 0.0)
    acc_ref[...] += _dot((u * u).astype(BF16), wd_ref[...])

    @pl.when(j == pl.num_programs(1) - 1)
    def _():
        y = x_ref[...] + mod_ref[0, 5:6, :] * acc_ref[...]
        if final:
            y = y * lax.rsqrt(jnp.mean(y * y, axis=-1, keepdims=True) + NORM_EPS) * fw_ref[...]
        o_ref[...] = y


def mlp(x2d, mod6, norm_w, w_up, w_down, final_w, seq_len, final, tm=1024, th=1024):
    m, d = x2d.shape
    hid = w_up.shape[1]
    tiles_per_seq = seq_len // tm
    return pl.pallas_call(
        functools.partial(_mlp_kernel, final=final),
        out_shape=jax.ShapeDtypeStruct((m, d), F32),
        grid=(m // tm, hid // th),
        in_specs=[
            pl.BlockSpec((tm, d), lambda i, j: (i, 0)),
            pl.BlockSpec((1, 6, d), lambda i, j: (i // tiles_per_seq, 0, 0)),
            pl.BlockSpec((1, d), lambda i, j: (0, 0)),
            pl.BlockSpec((d, th), lambda i, j: (0, j)),
            pl.BlockSpec((th, d), lambda i, j: (j, 0)),
            pl.BlockSpec((1, d), lambda i, j: (0, 0)),
        ],
        out_specs=pl.BlockSpec((tm, d), lambda i, j: (i, 0)),
        scratch_shapes=[pltpu.VMEM((tm, d), BF16), pltpu.VMEM((tm, d), F32)],
        compiler_params=_cparams(("parallel", "arbitrary")),
        name="mlp",
    )(x2d, mod6, norm_w.reshape(1, d), w_up, w_down, final_w.reshape(1, d))


def _split_w_in(w):
    d = w.shape[0]
    di = SSD_HEADS * SSD_HEAD_DIM
    cch = di + 2 * SSD_GROUPS * SSD_STATE
    sbw = SB_HEADS * HEAD_DIM
    dsw = DSA_HEADS * HEAD_DIM
    kvw = DSA_KV_HEADS * HEAD_DIM
    sizes = [di, cch, SSD_HEADS, sbw, sbw, sbw, dsw, kvw, kvw, IDX_HEADS * IDX_DIM, IDX_DIM, IDX_HEADS]
    offs = np.cumsum([0] + sizes)
    seg = [w[:, offs[i]:offs[i + 1]] for i in range(len(sizes))]
    gate = w[:, offs[-1]:]
    z_w, xbc_w, dt_w, sq, sk, sv, dq, dk, dv, iq, ik, iw = seg
    zpad = lambda a, n: jnp.pad(a, ((0, 0), (0, n - a.shape[1])))
    scale = HEAD_DIM ** -0.5
    w_ssd = jnp.concatenate([z_w, xbc_w, zpad(dt_w, LANES)], axis=1)
    w_sb = jnp.concatenate([sq * scale, sk, sv], axis=1)
    kv_pad = lambda a: jnp.concatenate(
        [zpad(a[:, g * HEAD_DIM:(g + 1) * HEAD_DIM], LANES) for g in range(DSA_KV_HEADS)], axis=1)
    w_dsa = jnp.concatenate([dq * scale, kv_pad(dk), kv_pad(dv), iq * (IDX_DIM ** -0.5),
                             zpad(ik, LANES), zpad(iw, LANES)], axis=1)
    return (w_ssd.astype(BF16), w_sb.astype(BF16), w_dsa.astype(BF16), gate.astype(BF16))


def kernel(x, c, norm1_w, ada_w, ada_b, w_in, conv_w, conv_b, dt_bias, a_log, d_skip, ssd_norm_w,
           w_br_ssd, w_br_sb, w_br_dsa, w_out, norm2_w, w_up, w_down, final_norm_w):
    bsz, seq, d = x.shape
    depth = ada_w.shape[0]
    m = bsz * seq
    di = SSD_HEADS * SSD_HEAD_DIM
    cch = di + 2 * SSD_GROUPS * SSD_STATE
    dsw = DSA_HEADS * HEAD_DIM
    kvl = DSA_KV_HEADS * LANES
    rope = rope_tables(seq)
    mods = ada_mod(c, ada_w, ada_b).reshape(depth, bsz, 6, d)
    x2d = x.reshape(m, d)
    for l in range(depth):
        mod6 = mods[l]
        w_ssd, w_sb, w_dsa, w_gate = _split_w_in(w_in[l])
        z, xbc, dt = proj(x2d, mod6, norm1_w[l], w_ssd, (di, cch, LANES), (F32, F32, F32), seq)
        (qkv,) = proj(x2d, mod6, norm1_w[l], w_sb, (3 * SB_HEADS * HEAD_DIM,), (BF16,), seq)
        dq, dk, dv, iq, ik, iw = proj(
            x2d, mod6, norm1_w[l], w_dsa, (dsw, kvl, kvl, dsw, LANES, LANES),
            (BF16, BF16, BF16, BF16, BF16, F32), seq, rope=rope,
            rope_chunks=(0, 1, 2, 3, 4, 5, 8, 9, 10, 11, 12))
        (gates,) = proj(x2d, mod6, norm1_w[l], w_gate, (3 * d,), (F32,), seq)

        r3 = lambda a: a.reshape(bsz, seq, a.shape[-1])
        y_ssd = ssd_mixer(r3(z), r3(xbc), r3(dt), conv_w[l], conv_b[l], dt_bias[l], a_log[l],
                          d_skip[l], ssd_norm_w[l])
        y_sb = sb_attention(r3(qkv), bsz, seq)
        y_dsa = dsa_attention(r3(dq), r3(dk), r3(dv), r3(iq), r3(ik), r3(iw), bsz, seq)

        x2d = merge(x2d, mod6, y_ssd.reshape(m, di), y_sb.reshape(m, -1), y_dsa.reshape(m, -1), gates,
                    w_br_ssd[l].astype(BF16), w_br_sb[l].astype(BF16), w_br_dsa[l].astype(BF16),
                    w_out[l].astype(BF16), seq)
        x2d = mlp(x2d, mod6, norm2_w[l], w_up[l].astype(BF16), w_down[l].astype(BF16), final_norm_w,
                  seq, final=(l == depth - 1))
    return x2d.reshape(bsz, seq, d)
```

```python
import functools
import math

import jax
import jax.numpy as jnp
import numpy as np
from jax import lax
from jax.experimental import pallas as pl
from jax.experimental.pallas import tpu as pltpu

F32 = jnp.float32
BF16 = jnp.bfloat16
I32 = jnp.int32

NORM_EPS = 1e-6
HEAD_DIM = 64
ROPE_THETA = 500000.0
LANES = 128

SSD_HEADS = 16
SSD_GROUPS = 2
SSD_STATE = 128
SSD_CHUNK = 128
SSD_HEAD_DIM = 64
SSD_CONV = 4

SB_HEADS = 8
DSA_HEADS = 8
DSA_KV_HEADS = 2
IDX_HEADS = 8
IDX_DIM = 64
DSA_MAX_TOPK = 256

ATT_BLOCK = 256
SB_Q_BLOCK = 512
VMEM_LIMIT = 56 * 1024 * 1024

LOG2E = 1.4426950408889634
INT_MIN = -(2 ** 31)
NEG_BIG = -1e30


def _cparams(sem):
    return pltpu.CompilerParams(dimension_semantics=sem, vmem_limit_bytes=VMEM_LIMIT)


def _dot(a, b):
    return jnp.dot(a, b, preferred_element_type=F32)


def _dot_nt(a, b):
    return lax.dot_general(a, b, (((1,), (1,)), ((), ())), preferred_element_type=F32)


def _softplus(x):
    return jnp.maximum(x, 0.0) + jnp.log1p(jnp.exp(-jnp.abs(x)))


def _norm_mod(x, w, scale, shift):
    y = x * lax.rsqrt(jnp.mean(x * x, axis=-1, keepdims=True) + NORM_EPS)
    return (y * w) * (1.0 + scale) + shift


def _ada_kernel(c_ref, w_ref, b_ref, o_ref):
    c = c_ref[...]
    a = (c * jax.nn.sigmoid(c)).astype(BF16)
    o_ref[0] = _dot(a, w_ref[0].astype(BF16)) + b_ref[0]


def ada_mod(c, ada_w, ada_b, tn=1536):
    depth, d, n = ada_w.shape
    bsz = c.shape[0]
    return pl.pallas_call(
        _ada_kernel,
        out_shape=jax.ShapeDtypeStruct((depth, bsz, n), F32),
        grid=(depth, n // tn),
        in_specs=[
            pl.BlockSpec((bsz, d), lambda l, j: (0, 0)),
            pl.BlockSpec((1, d, tn), lambda l, j: (l, 0, j)),
            pl.BlockSpec((1, 1, tn), lambda l, j: (l, 0, j)),
        ],
        out_specs=pl.BlockSpec((1, bsz, tn), lambda l, j: (l, 0, j)),
        compiler_params=_cparams(("parallel", "parallel")),
        name="ada_mod",
    )(c, ada_w, ada_b.reshape(depth, 1, n))


def _proj_kernel(*refs, splits, rope_chunks, has_rope):
    if has_rope:
        x_ref, mod_ref, nw_ref, w_ref, cos_ref, s1_ref, s2_ref = refs[:7]
        out_refs = refs[7:]
    else:
        x_ref, mod_ref, nw_ref, w_ref = refs[:4]
        out_refs = refs[4:]
    h = _norm_mod(x_ref[...], nw_ref[...], mod_ref[0, 1:2, :], mod_ref[0, 0:1, :]).astype(BF16)
    off = 0
    for o_ref, width in zip(out_refs, splits):
        acc = _dot(h, w_ref[:, off:off + width])
        if has_rope:
            cos, s1, s2 = cos_ref[...], s1_ref[...], s2_ref[...]
            pieces = []
            for c in range(width // LANES):
                xc = acc[:, c * LANES:(c + 1) * LANES]
                if (off // LANES + c) in rope_chunks:
                    xc = (xc * cos + pltpu.roll(xc, LANES - 8, axis=1) * s1
                          + pltpu.roll(xc, 8, axis=1) * s2)
                pieces.append(xc)
            acc = pieces[0] if len(pieces) == 1 else jnp.concatenate(pieces, axis=1)
        o_ref[...] = acc.astype(o_ref.dtype)
        off += width


def proj(x2d, mod6, norm_w, w, splits, dtypes, seq_len, rope=None, rope_chunks=(), tm=512):
    m, d = x2d.shape
    n = w.shape[1]
    assert sum(splits) == n and m % tm == 0 and seq_len % tm == 0
    tiles_per_seq = seq_len // tm
    in_specs = [
        pl.BlockSpec((tm, d), lambda i: (i, 0)),
        pl.BlockSpec((1, 6, d), lambda i: (i // tiles_per_seq, 0, 0)),
        pl.BlockSpec((1, d), lambda i: (0, 0)),
        pl.BlockSpec((d, n), lambda i: (0, 0)),
    ]
    args = [x2d, mod6, norm_w.reshape(1, d), w]
    if rope is not None:
        for t in rope:
            in_specs.append(pl.BlockSpec((tm, LANES), lambda i: (i % tiles_per_seq, 0)))
            args.append(t)
    kern = functools.partial(_proj_kernel, splits=tuple(splits), rope_chunks=tuple(rope_chunks),
                             has_rope=rope is not None)
    return pl.pallas_call(
        kern,
        out_shape=[jax.ShapeDtypeStruct((m, s), dt) for s, dt in zip(splits, dtypes)],
        grid=(m // tm,),
        in_specs=in_specs,
        out_specs=[pl.BlockSpec((tm, s), lambda i: (i, 0)) for s in splits],
        compiler_params=_cparams(("parallel",)),
        name="proj",
    )(*args)


def rope_tables(seq_len):
    rot = HEAD_DIM // 4
    half = rot // 2
    inv_freq = jnp.exp(jnp.arange(half, dtype=F32) * (-2.0 * math.log(ROPE_THETA) / rot))
    ang = jnp.arange(seq_len, dtype=jnp.int32).astype(F32)[:, None] * inv_freq[None, :]
    cos, sin = jnp.cos(ang), jnp.sin(ang)
    ones = jnp.ones((seq_len, HEAD_DIM - rot), F32)
    zeros = jnp.zeros((seq_len, HEAD_DIM - rot), F32)
    zh = jnp.zeros((seq_len, half), F32)
    c64 = jnp.concatenate([cos, cos, ones], axis=1)
    s1_64 = jnp.concatenate([-sin, zh, zeros], axis=1)
    s2_64 = jnp.concatenate([zh, sin, zeros], axis=1)
    tile = lambda a: jnp.concatenate([a, a], axis=1)
    return tile(c64), tile(s1_64), tile(s2_64)


def _expand_heads(a):
    q = a.shape[0]
    lane = lax.broadcasted_iota(I32, (q, LANES), 1)
    chunks = []
    for c in range(SSD_HEADS // 2):
        lo = jnp.broadcast_to(a[:, 2 * c:2 * c + 1], (q, LANES))
        hi = jnp.broadcast_to(a[:, 2 * c + 1:2 * c + 2], (q, LANES))
        chunks.append(jnp.where(lane < SSD_HEAD_DIM, lo, hi))
    return jnp.concatenate(chunks, axis=1)


def _ssd_kernel(z_ref, xbc_ref, dt_ref, cw_ref, cb_ref, dtb_ref, alog_ref, dskip_ref, nw_ref,
                o_ref, state_ref, tail_ref):
    q = SSD_CHUNK
    di = SSD_HEADS * SSD_HEAD_DIM
    gw = SSD_STATE
    hpg = SSD_HEADS // SSD_GROUPS

    @pl.when(pl.program_id(1) == 0)
    def _():
        state_ref[...] = jnp.zeros_like(state_ref)
        tail_ref[...] = jnp.zeros_like(tail_ref)

    cur = xbc_ref[0]
    row = lax.broadcasted_iota(I32, (q, 1), 0)
    tail = tail_ref[...]
    conv = cb_ref[...] + cw_ref[SSD_CONV - 1:SSD_CONV, :] * cur
    for j in range(1, SSD_CONV):
        rolled = pltpu.roll(cur, j, axis=0)
        patch = jnp.tile(pltpu.roll(tail, j, axis=0), (q // 8, 1))
        shifted = jnp.where(row < j, patch, rolled)
        conv = conv + cw_ref[SSD_CONV - 1 - j:SSD_CONV - j, :] * shifted
    tail_ref[...] = cur[q - 8:, :]
    xbc = conv * jax.nn.sigmoid(conv)
    xs = xbc[:, :di]
    bm = xbc[:, di:di + SSD_GROUPS * gw]
    cm = xbc[:, di + SSD_GROUPS * gw:]

    dt = _softplus(dt_ref[0] + dtb_ref[...])
    da = dt * (-jnp.exp(alog_ref[...]))
    acum = da
    s = 1
    while s < q:
        acum = acum + jnp.where(row >= s, pltpu.roll(acum, s, axis=0), 0.0)
        s *= 2
    acum_t = acum.T
    dt_x = _expand_heads(dt)
    acum_x = _expand_heads(acum)
    a_last_x = acum_x[q - 1:q, :]
    e_in_x = jnp.exp(acum_x)
    e_out_x = jnp.exp(a_last_x - acum_x)
    e_last_x = jnp.exp(a_last_x)

    xdt = xs * dt_x
    xdt_b = xdt.astype(BF16)
    xdt_out_b = (xdt * e_out_x).astype(BF16)

    ri = lax.broadcasted_iota(I32, (q, q), 0)
    ci = lax.broadcasted_iota(I32, (q, q), 1)
    causal = ri >= ci

    y_parts = []
    new_states = []
    for g in range(SSD_GROUPS):
        bm_g = bm[:, g * gw:(g + 1) * gw]
        cm_g = cm[:, g * gw:(g + 1) * gw].astype(BF16)
        bm_gb = bm_g.astype(BF16)
        cb = _dot_nt(cm_g, bm_gb)
        sl = slice(g * hpg * SSD_HEAD_DIM, (g + 1) * hpg * SSD_HEAD_DIM)
        st_g = state_ref[:, sl]
        y_off = _dot(cm_g, st_g.astype(BF16)) * e_in_x[:, sl]
        for r in range(hpg):
            h = g * hpg + r
            diff = acum[:, h:h + 1] - acum_t[h:h + 1, :]
            lmat = jnp.exp(jnp.where(causal, diff, NEG_BIG))
            m_h = (cb * lmat).astype(BF16)
            y_parts.append(_dot(m_h, xdt_b[:, h * SSD_HEAD_DIM:(h + 1) * SSD_HEAD_DIM])
                           + y_off[:, r * SSD_HEAD_DIM:(r + 1) * SSD_HEAD_DIM])
        upd = _dot(bm_g.T.astype(BF16), xdt_out_b[:, sl])
        new_states.append(st_g * e_last_x[:, sl] + upd)
    for g in range(SSD_GROUPS):
        sl = slice(g * hpg * SSD_HEAD_DIM, (g + 1) * hpg * SSD_HEAD_DIM)
        state_ref[:, sl] = new_states[g]

    y = jnp.concatenate(y_parts, axis=1) + xs * _expand_heads(dskip_ref[...])
    zg = z_ref[0]
    y = y * (zg * jax.nn.sigmoid(zg))
    y = y * lax.rsqrt(jnp.mean(y * y, axis=-1, keepdims=True) + NORM_EPS) * nw_ref[...]
    o_ref[0] = y.astype(o_ref.dtype)


def ssd_mixer(z, xbc, dt, conv_w, conv_b, dt_bias, a_log, d_skip, norm_w):
    bsz, seq, di = z.shape
    cch = xbc.shape[-1]
    nc = seq // SSD_CHUNK
    pad = lambda v: jnp.pad(v.astype(F32), (0, LANES - v.shape[0])).reshape(1, LANES)
    full = lambda shape: pl.BlockSpec(shape, lambda b, c: (0,) * len(shape))
    return pl.pallas_call(
        _ssd_kernel,
        out_shape=jax.ShapeDtypeStruct((bsz, seq, di), BF16),
        grid=(bsz, nc),
        in_specs=[
            pl.BlockSpec((1, SSD_CHUNK, di), lambda b, c: (b, c, 0)),
            pl.BlockSpec((1, SSD_CHUNK, cch), lambda b, c: (b, c, 0)),
            pl.BlockSpec((1, SSD_CHUNK, LANES), lambda b, c: (b, c, 0)),
            full((SSD_CONV, cch)), full((1, cch)),
            full((1, LANES)), full((1, LANES)), full((1, LANES)), full((1, di)),
        ],
        out_specs=pl.BlockSpec((1, SSD_CHUNK, di), lambda b, c: (b, c, 0)),
        scratch_shapes=[pltpu.VMEM((SSD_STATE, di), F32), pltpu.VMEM((8, cch), F32)],
        compiler_params=_cparams(("parallel", "arbitrary")),
        name="ssd",
    )(z, xbc, dt, conv_w, conv_b.reshape(1, cch), pad(dt_bias), pad(a_log), pad(d_skip),
      norm_w.reshape(1, di))


def _sb_kernel(q_ref, k_ref, v_ref, o_ref, acc_ref, carry_ref):
    tq, tk = SB_Q_BLOCK, ATT_BLOCK
    qi = pl.program_id(2)
    sr = lax.broadcasted_iota(I32, (tk, tk), 0)
    sc = lax.broadcasted_iota(I32, (tk, tk), 1)
    strict_late = sc < sr
    fr = lax.broadcasted_iota(I32, (tq, tk), 0)
    fc = lax.broadcasted_iota(I32, (tq, tk), 1)
    strict_early = jnp.logical_or(fc < fr, fr >= tk)
    suffix = jnp.where(sr > sc, 1.0, 0.0).astype(BF16)

    acc_ref[...] = jnp.zeros_like(acc_ref)
    carry_ref[...] = jnp.zeros_like(carry_ref)

    def block(j, r0, nr, strict):
        masked = strict is not None
        start = pl.multiple_of(j * tk, tk)
        rows = slice(r0, r0 + nr)
        for hh in range(2):
            lanes = slice(hh * HEAD_DIM, (hh + 1) * HEAD_DIM)
            kj = k_ref[0, pl.ds(start, tk), lanes]
            vj = v_ref[0, pl.ds(start, tk), lanes]
            z = _dot_nt(q_ref[0, rows, lanes], kj) * LOG2E
            sp = jnp.maximum(z, 0.0) + jnp.log2(1.0 + jnp.exp2(-jnp.abs(z)))
            spm = jnp.where(strict, sp, 0.0) if masked else sp
            carry = carry_ref[hh, rows, :]
            after = _dot(spm.astype(BF16), suffix) + jnp.concatenate([carry, carry], axis=1)
            att = jnp.exp2(z - sp - after)
            if masked:
                att = jnp.where(strict, att, 0.0)
            acc_ref[hh, rows, :] += _dot(att.astype(BF16), vj)
            carry_ref[hh, rows, :] = jnp.broadcast_to(after[:, 0:1] + spm[:, 0:1], (nr, LANES))

    nkb = (qi + 1) * (tq // tk)
    block(nkb - 1, tk, tk, strict_late)
    block(nkb - 2, 0, tq, strict_early)

    def body(i, c):
        for d in range(tq // tk):
            block((qi - i) * (tq // tk) - 1 - d, 0, tq, None)
        return c

    lax.fori_loop(0, qi, body, 0)
    o_ref[0] = jnp.concatenate([acc_ref[0], acc_ref[1]], axis=1).astype(o_ref.dtype)


def sb_attention(qkv, bsz, seq):
    width = SB_HEADS * HEAD_DIM
    npair = width // LANES
    nq = seq // SB_Q_BLOCK
    assert SB_Q_BLOCK == 2 * ATT_BLOCK and seq % SB_Q_BLOCK == 0
    return pl.pallas_call(
        _sb_kernel,
        out_shape=jax.ShapeDtypeStruct((bsz, seq, width), BF16),
        grid=(bsz, npair, nq),
        in_specs=[
            pl.BlockSpec((1, SB_Q_BLOCK, LANES), lambda b, p, i: (b, i, p)),
            pl.BlockSpec((1, seq, LANES), lambda b, p, i: (b, 0, npair + p)),
            pl.BlockSpec((1, seq, LANES), lambda b, p, i: (b, 0, 2 * npair + p)),
        ],
        out_specs=pl.BlockSpec((1, SB_Q_BLOCK, LANES), lambda b, p, i: (b, i, p)),
        scratch_shapes=[pltpu.VMEM((2, SB_Q_BLOCK, HEAD_DIM), F32),
                        pltpu.VMEM((2, SB_Q_BLOCK, LANES), F32)],
        compiler_params=_cparams(("parallel", "parallel", "arbitrary")),
        name="stickbreak",
    )(qkv, qkv, qkv)


def _fold_rows(m):
    parts = [m[i * 8:(i + 1) * 8, :] for i in range(m.shape[0] // 8)]
    while len(parts) > 1:
        parts = [a + b for a, b in zip(parts[0::2], parts[1::2])]
    return parts[0]


def _ordered_to_float(u):
    key = u ^ INT_MIN
    bits = jnp.where(key < 0, key ^ 0x7FFFFFFF, key)
    return pltpu.bitcast(bits, F32)


def _dsa_kernel(dq_ref, dk_ref, dv_ref, iq_ref, ik_ref, iw_ref, o_ref,
                sc_ref, bias_ref, qs_ref, s_ref, mx_ref, l_ref, acc_ref):
    t = ATT_BLOCK
    qi = pl.program_id(1)
    nkb = qi + 1
    rpg = DSA_HEADS // DSA_KV_HEADS
    topk = float(DSA_MAX_TOPK)
    si = lax.broadcasted_iota(I32, (t, t), 0)
    ti = lax.broadcasted_iota(I32, (t, t), 1)

    qs_ref[:, HEAD_DIM:] = jnp.zeros((qs_ref.shape[0], LANES - HEAD_DIM), BF16)
    for h in range(IDX_HEADS):
        qs_ref[h * t:(h + 1) * t, 0:IDX_DIM] = iq_ref[0, :, h * IDX_DIM:(h + 1) * IDX_DIM]
    wt = (iw_ref[0] * (IDX_HEADS ** -0.5)).T

    def score_block(j, c):
        start = pl.multiple_of(j * t, t)
        ikj = ik_ref[0, pl.ds(start, t), :]
        logits = _dot_nt(ikj, qs_ref[...])
        score = jnp.zeros((t, t), F32)
        for h in range(IDX_HEADS):
            score = score + wt[h:h + 1, :] * jnp.maximum(logits[:, h * t:(h + 1) * t], 0.0)
        causal = (j * t + si) <= (qi * t + ti)
        sc_ref[j] = jnp.where(causal, score, -jnp.inf)
        return c

    lax.fori_loop(0, nkb, score_block, 0)

    def count_ge(cand):
        def body(j, cnt):
            return cnt + _fold_rows(jnp.where(sc_ref[j] >= cand, 1.0, 0.0))
        cnt = lax.fori_loop(0, nkb, body, jnp.zeros((8, t), F32))
        return jnp.sum(cnt, axis=0, keepdims=True)

    @pl.when(qi == 0)
    def _():
        bias_ref[0] = jnp.where(ti <= si, 0.0, NEG_BIG)

    @pl.when(qi > 0)
    def _():
        def bit_step(i, c):
            thr_u, cnt_thr = c
            cand_u = thr_u | lax.shift_left(jnp.int32(1), 31 - i)
            cnt = count_ge(_ordered_to_float(cand_u))
            ok = cnt >= topk
            return jnp.where(ok, cand_u, thr_u), jnp.where(ok, cnt, cnt_thr)

        thr_u, cnt_thr = lax.fori_loop(
            0, 32, bit_step, (jnp.zeros((1, t), I32), jnp.full((1, t), topk, F32)))
        thr = _ordered_to_float(thr_u)
        excess = jnp.max(cnt_thr) > topk

        @pl.when(jnp.logical_not(excess))
        def _():
            def fill(j, c):
                bias_ref[j] = jnp.where(sc_ref[j] >= thr, 0.0, NEG_BIG).T
                return c
            lax.fori_loop(0, nkb, fill, 0)

        @pl.when(excess)
        def _():
            nxt = _ordered_to_float(thr_u + 1)
            need = topk - count_ge(nxt)
            incl = jnp.where(ti <= si, 1.0, 0.0).astype(BF16)

            def fix(j, seen):
                sc = sc_ref[j]
                gt = sc >= nxt
                eq = jnp.logical_and(sc >= thr, jnp.logical_not(gt))
                rank = _dot(incl, jnp.where(eq, 1.0, 0.0).astype(BF16)) + seen
                sel = jnp.logical_or(gt, jnp.logical_and(eq, rank <= need))
                bias_ref[j] = jnp.where(sel, 0.0, NEG_BIG).T
                return rank[t - 1:t, :]

            lax.fori_loop(0, nkb, fix, jnp.zeros((1, t), F32))

    for h in range(DSA_HEADS):
        qs_ref[h * t:(h + 1) * t, 0:HEAD_DIM] = dq_ref[0, :, h * HEAD_DIM:(h + 1) * HEAD_DIM]
    mx_ref[...] = jnp.full(mx_ref.shape, NEG_BIG, F32)
    grp = [(slice(g * rpg * t, (g + 1) * rpg * t), slice(g * LANES, (g + 1) * LANES))
           for g in range(DSA_KV_HEADS)]

    def pass_scores(j, c):
        start = pl.multiple_of(j * t, t)
        bias = jnp.concatenate([bias_ref[j]] * rpg, axis=0)
        for rows, lanes in grp:
            kj = dk_ref[0, pl.ds(start, t), lanes]
            s = _dot_nt(qs_ref[rows, :], kj) + bias
            s_ref[j, rows, :] = s
            mx_ref[rows, :] = jnp.maximum(mx_ref[rows, :], jnp.maximum(s[:, :LANES], s[:, LANES:]))
        return c

    lax.fori_loop(0, nkb, pass_scores, 0)
    m = jnp.max(mx_ref[...], axis=1, keepdims=True)
    mx_ref[...] = jnp.broadcast_to(m, mx_ref.shape)
    l_ref[...] = jnp.zeros_like(l_ref)
    acc_ref[...] = jnp.zeros_like(acc_ref)

    def pass_values(j, c):
        start = pl.multiple_of(j * t, t)
        for rows, lanes in grp:
            vj = dv_ref[0, pl.ds(start, t), lanes]
            mb = mx_ref[rows, :]
            p = jnp.exp(s_ref[j, rows, :] - jnp.concatenate([mb, mb], axis=1))
            l_ref[rows, :] += p[:, :LANES] + p[:, LANES:]
            acc_ref[rows, :] += _dot(p.astype(BF16), vj)
        return c

    lax.fori_loop(0, nkb, pass_values, 0)
    out = acc_ref[...] / jnp.sum(l_ref[...], axis=1, keepdims=True)
    for h in range(DSA_HEADS):
        o_ref[0, :, h * HEAD_DIM:(h + 1) * HEAD_DIM] = out[h * t:(h + 1) * t, 0:HEAD_DIM].astype(o_ref.dtype)


def dsa_attention(dq, dk, dv, iq, ik, iw, bsz, seq):
    t = ATT_BLOCK
    nq = seq // t
    assert IDX_HEADS == DSA_HEADS and IDX_DIM == HEAD_DIM
    qblk = lambda w: pl.BlockSpec((1, t, w), lambda b, i: (b, i, 0))
    kblk = lambda w: pl.BlockSpec((1, seq, w), lambda b, i: (b, 0, 0))
    return pl.pallas_call(
        _dsa_kernel,
        out_shape=jax.ShapeDtypeStruct((bsz, seq, DSA_HEADS * HEAD_DIM), BF16),
        grid=(bsz, nq),
        in_specs=[qblk(DSA_HEADS * HEAD_DIM), kblk(DSA_KV_HEADS * LANES), kblk(DSA_KV_HEADS * LANES),
                  qblk(IDX_HEADS * IDX_DIM), kblk(LANES), qblk(LANES)],
        out_specs=qblk(DSA_HEADS * HEAD_DIM),
        scratch_shapes=[
            pltpu.VMEM((nq, t, t), F32),
            pltpu.VMEM((nq, t, t), F32),
            pltpu.VMEM((DSA_HEADS * t, LANES), BF16),
            pltpu.VMEM((nq, DSA_HEADS * t, t), F32),
            pltpu.VMEM((DSA_HEADS * t, LANES), F32),
            pltpu.VMEM((DSA_HEADS * t, LANES), F32),
            pltpu.VMEM((DSA_HEADS * t, LANES), F32),
        ],
        compiler_params=_cparams(("parallel", "arbitrary")),
        name="dsa",
    )(dq, dk, dv, iq, ik, iw)


def _merge_kernel(x_ref, mod_ref, ys_ref, yb_ref, yd_ref, g0_ref, g1_ref, g2_ref,
                  ws_ref, wb_ref, wd_ref, wo_ref, o_ref):
    merged = (jax.nn.sigmoid(g0_ref[...]) * _dot(ys_ref[...], ws_ref[...])
              + jax.nn.sigmoid(g1_ref[...]) * _dot(yb_ref[...], wb_ref[...])
              + jax.nn.sigmoid(g2_ref[...]) * _dot(yd_ref[...], wd_ref[...]))
    o_ref[...] = x_ref[...] + mod_ref[0, 2:3, :] * _dot(merged.astype(BF16), wo_ref[...])


def merge(x2d, mod6, y_ssd, y_sb, y_dsa, gates, w_ssd, w_sb, w_dsa, w_out, seq_len, tm=512):
    m, d = x2d.shape
    tiles_per_seq = seq_len // tm
    row = lambda w: pl.BlockSpec((tm, w), lambda i: (i, 0))
    full = lambda a: pl.BlockSpec(a.shape, lambda i: (0, 0))
    return pl.pallas_call(
        _merge_kernel,
        out_shape=jax.ShapeDtypeStruct((m, d), F32),
        grid=(m // tm,),
        in_specs=[
            row(d),
            pl.BlockSpec((1, 6, d), lambda i: (i // tiles_per_seq, 0, 0)),
            row(y_ssd.shape[1]), row(y_sb.shape[1]), row(y_dsa.shape[1]),
            pl.BlockSpec((tm, d), lambda i: (i, 0)),
            pl.BlockSpec((tm, d), lambda i: (i, 1)),
            pl.BlockSpec((tm, d), lambda i: (i, 2)),
            full(w_ssd), full(w_sb), full(w_dsa), full(w_out),
        ],
        out_specs=row(d),
        compiler_params=_cparams(("parallel",)),
        name="merge",
    )(x2d, mod6, y_ssd, y_sb, y_dsa, gates, gates, gates, w_ssd, w_sb, w_dsa, w_out)


def _mlp_kernel(x_ref, mod_ref, nw_ref, wu_ref, wd_ref, fw_ref, o_ref, h_ref, acc_ref, *, final):
    j = pl.program_id(1)

    @pl.when(j == 0)
    def _():
        h_ref[...] = _norm_mod(x_ref[...], nw_ref[...], mod_ref[0, 4:5, :],
                               mod_ref[0, 3:4, :]).astype(BF16)
        acc_ref[...] = jnp.zeros_like(acc_ref)

    u = jnp.maximum(_dot(h_ref[...], wu_ref[...]), 0.0)
    acc_ref[...] += _dot((u * u).astype(BF16), wd_ref[...])

    @pl.when(j == pl.num_programs(1) - 1)
    def _():
        y = x_ref[...] + mod_ref[0, 5:6, :] * acc_ref[...]
        if final:
            y = y * lax.rsqrt(jnp.mean(y * y, axis=-1, keepdims=True) + NORM_EPS) * fw_ref[...]
        o_ref[...] = y


def mlp(x2d, mod6, norm_w, w_up, w_down, final_w, seq_len, final, tm=1024, th=1024):
    m, d = x2d.shape
    hid = w_up.shape[1]
    tiles_per_seq = seq_len // tm
    return pl.pallas_call(
        functools.partial(_mlp_kernel, final=final),
        out_shape=jax.ShapeDtypeStruct((m, d), F32),
        grid=(m // tm, hid // th),
        in_specs=[
            pl.BlockSpec((tm, d), lambda i, j: (i, 0)),
            pl.BlockSpec((1, 6, d), lambda i, j: (i // tiles_per_seq, 0, 0)),
            pl.BlockSpec((1, d), lambda i, j: (0, 0)),
            pl.BlockSpec((d, th), lambda i, j: (0, j)),
            pl.BlockSpec((th, d), lambda i, j: (j, 0)),
            pl.BlockSpec((1, d), lambda i, j: (0, 0)),
        ],
        out_specs=pl.BlockSpec((tm, d), lambda i, j: (i, 0)),
        scratch_shapes=[pltpu.VMEM((tm, d), BF16), pltpu.VMEM((tm, d), F32)],
        compiler_params=_cparams(("parallel", "arbitrary")),
        name="mlp",
    )(x2d, mod6, norm_w.reshape(1, d), w_up, w_down, final_w.reshape(1, d))


def _split_w_in(w):
    d = w.shape[0]
    di = SSD_HEADS * SSD_HEAD_DIM
    cch = di + 2 * SSD_GROUPS * SSD_STATE
    sbw = SB_HEADS * HEAD_DIM
    dsw = DSA_HEADS * HEAD_DIM
    kvw = DSA_KV_HEADS * HEAD_DIM
    sizes = [di, cch, SSD_HEADS, sbw, sbw, sbw, dsw, kvw, kvw, IDX_HEADS * IDX_DIM, IDX_DIM, IDX_HEADS]
    offs = np.cumsum([0] + sizes)
    seg = [w[:, offs[i]:offs[i + 1]] for i in range(len(sizes))]
    gate = w[:, offs[-1]:]
    z_w, xbc_w, dt_w, sq, sk, sv, dq, dk, dv, iq, ik, iw = seg
    zpad = lambda a, n: jnp.pad(a, ((0, 0), (0, n - a.shape[1])))
    scale = HEAD_DIM ** -0.5
    w_ssd = jnp.concatenate([z_w, xbc_w, zpad(dt_w, LANES)], axis=1)
    w_sb = jnp.concatenate([sq * scale, sk, sv], axis=1)
    kv_pad = lambda a: jnp.concatenate(
        [zpad(a[:, g * HEAD_DIM:(g + 1) * HEAD_DIM], LANES) for g in range(DSA_KV_HEADS)], axis=1)
    w_dsa = jnp.concatenate([dq * scale, kv_pad(dk), kv_pad(dv), iq * (IDX_DIM ** -0.5),
                             zpad(ik, LANES), zpad(iw, LANES)], axis=1)
    return (w_ssd.astype(BF16), w_sb.astype(BF16), w_dsa.astype(BF16), gate.astype(BF16))


def kernel(x, c, norm1_w, ada_w, ada_b, w_in, conv_w, conv_b, dt_bias, a_log, d_skip, ssd_norm_w,
           w_br_ssd, w_br_sb, w_br_dsa, w_out, norm2_w, w_up, w_down, final_norm_w):
    bsz, seq, d = x.shape
    depth = ada_w.shape[0]
    m = bsz * seq
    di = SSD_HEADS * SSD_HEAD_DIM
    cch = di + 2 * SSD_GROUPS * SSD_STATE
    dsw = DSA_HEADS * HEAD_DIM
    kvl = DSA_KV_HEADS * LANES
    rope = rope_tables(seq)
    mods = ada_mod(c, ada_w, ada_b).reshape(depth, bsz, 6, d)
    x2d = x.reshape(m, d)
    for l in range(depth):
        mod6 = mods[l]
        w_ssd, w_sb, w_dsa, w_gate = _split_w_in(w_in[l])
        z, xbc, dt = proj(x2d, mod6, norm1_w[l], w_ssd, (di, cch, LANES), (F32, F32, F32), seq)
        (qkv,) = proj(x2d, mod6, norm1_w[l], w_sb, (3 * SB_HEADS * HEAD_DIM,), (BF16,), seq)
        dq, dk, dv, iq, ik, iw = proj(
            x2d, mod6, norm1_w[l], w_dsa, (dsw, kvl, kvl, dsw, LANES, LANES),
            (BF16, BF16, BF16, BF16, BF16, F32), seq, rope=rope,
            rope_chunks=(0, 1, 2, 3, 4, 5, 8, 9, 10, 11, 12))
        (gates,) = proj(x2d, mod6, norm1_w[l], w_gate, (3 * d,), (F32,), seq)

        r3 = lambda a: a.reshape(bsz, seq, a.shape[-1])
        y_ssd = ssd_mixer(r3(z), r3(xbc), r3(dt), conv_w[l], conv_b[l], dt_bias[l], a_log[l],
                          d_skip[l], ssd_norm_w[l])
        y_sb = sb_attention(r3(qkv), bsz, seq)
        y_dsa = dsa_attention(r3(dq), r3(dk), r3(dv), r3(iq), r3(ik), r3(iw), bsz, seq)

        x2d = merge(x2d, mod6, y_ssd.reshape(m, di), y_sb.reshape(m, -1), y_dsa.reshape(m, -1), gates,
                    w_br_ssd[l].astype(BF16), w_br_sb[l].astype(BF16), w_br_dsa[l].astype(BF16),
                    w_out[l].astype(BF16), seq)
        x2d = mlp(x2d, mod6, norm2_w[l], w_up[l].astype(BF16), w_down[l].astype(BF16), final_norm_w,
                  seq, final=(l == depth - 1))
    return x2d.reshape(bsz, seq, d)
```
